```python
import math
import jax
import jax.numpy as jnp
from jax import lax
import numpy as np

D_MODEL = 1024
BATCH = 32
SEQ = 256
DEPTH = 4
DEC_BATCH = 4
DEC_SEQ = 4096
PAST_LEN = 256

GRID_W = 64
EPS = 1e-6
ROPE_BASE = 10000.0
BLOCK = 128
A_HEADS = 8
A_KV_HEADS = 2
A_GROUP = A_HEADS // A_KV_HEADS
A_HEAD_DIM = 64
A_WIDTH = A_HEADS * A_HEAD_DIM
WINDOW = 128
SSM_CH = 16
SSM_GROUPS = 16
SSM_WIDTH = SSM_GROUPS * SSM_CH
SSM_STATE = 64
MLA_HEADS = 4
MLA_NOPE = 64
MLA_ROPE = 32
MLA_QK_DIM = MLA_NOPE + MLA_ROPE
MLA_V = 64
MLA_Q_LORA = 256
MLA_KV_LORA = 128
MLA_WIDTH = MLA_HEADS * MLA_V
D_MIX = A_WIDTH + SSM_WIDTH + MLA_WIDTH
IN_SIZES = (A_WIDTH, A_KV_HEADS * A_HEAD_DIM, A_KV_HEADS * A_HEAD_DIM, SSM_WIDTH, MLA_Q_LORA, MLA_KV_LORA, MLA_ROPE)
N_IN = A_WIDTH + 2 * A_KV_HEADS * A_HEAD_DIM + SSM_WIDTH + MLA_Q_LORA + MLA_KV_LORA + MLA_ROPE
N_EXPERTS = 16
N_EXPERT_GROUPS = 4
EXPERTS_PER_GROUP = N_EXPERTS // N_EXPERT_GROUPS
TOP_K = 2
GROUP_SCORE_K = 2
D_EXPERT = 512

kernel_name = 'hybrid_flow_trunk_step'


def rms_norm(x, g):
    xf = x.astype(jnp.float32)
    y = xf * lax.rsqrt(jnp.mean(xf * xf, axis=-1, keepdims=True) + EPS)
    return (y * g.astype(jnp.float32)).astype(x.dtype)


def rope_1d(x, pos):
    half = x.shape[-1] // 2
    inv = ROPE_BASE ** (-jnp.arange(half, dtype=jnp.float32) / half)
    ang = pos.astype(jnp.float32)[:, None] * inv[None, :]
    cos = jnp.cos(ang)[None, :, None, :]
    sin = jnp.sin(ang)[None, :, None, :]
    xf = x.astype(jnp.float32)
    x1, x2 = xf[..., :half], xf[..., half:]
    return jnp.concatenate([x1 * cos - x2 * sin, x2 * cos + x1 * sin], axis=-1).astype(x.dtype)


def rope_2d(x, row_pos, col_pos):
    h = x.shape[-1] // 2
    return jnp.concatenate([rope_1d(x[..., :h], row_pos), rope_1d(x[..., h:], col_pos)], axis=-1)


def grid_positions(n_tokens):
    rows = n_tokens // GRID_W
    return jnp.repeat(jnp.arange(rows), GRID_W), jnp.tile(jnp.arange(GRID_W), rows)


def softmax_with_sink(logits, sink):
    sink = jnp.broadcast_to(sink.astype(jnp.float32), logits.shape[:-1] + (1,))
    return jax.nn.softmax(jnp.concatenate([logits, sink], axis=-1), axis=-1)[..., :-1]


def dense_attention(q, k, v, sink):
    B, L, Hk, G, d = q.shape
    nb = L // BLOCK
    scale = d ** -0.5
    q_blocks = jnp.moveaxis(q.reshape(B, nb, BLOCK, Hk, G, d), 1, 0)

    def one_block(qb):
        s = jnp.einsum('bqhgd,bkhd->bhgqk', qb, k).astype(jnp.float32) * scale
        if sink is None:
            p = jax.nn.softmax(s, axis=-1)
        else:
            p = softmax_with_sink(s, sink[None, :, :, None, None])
        return jnp.einsum('bhgqk,bkhd->bqhgd', p.astype(v.dtype), v)

    out = lax.map(one_block, q_blocks)
    return jnp.moveaxis(out, 0, 1).reshape(B, L, Hk * G * v.shape[-1])


def window_attention(q, k, v, k_ctx, v_ctx, sink):
    B, L, Hk, G, d = q.shape
    nb = L // BLOCK
    scale = d ** -0.5

    def band(t):
        tp = jnp.pad(t, ((0, 0), (BLOCK, BLOCK), (0, 0), (0, 0)))
        tb = tp.reshape(B, nb + 2, BLOCK, Hk, t.shape[-1])
        return jnp.concatenate([tb[:, :-2], tb[:, 1:-1], tb[:, 2:]], axis=2)

    k_band, v_band = band(k), band(v)
    qb = q.reshape(B, nb, BLOCK, Hk, G, d)
    s_loc = jnp.einsum('bnqhgd,bnjhd->bnhgqj', qb, k_band).astype(jnp.float32) * scale
    s_ctx = jnp.einsum('bnqhgd,bchd->bnhgqc', qb, k_ctx).astype(jnp.float32) * scale
    q_pos = jnp.arange(nb)[:, None] * BLOCK + jnp.arange(BLOCK)[None, :]
    k_pos = jnp.arange(nb)[:, None] * BLOCK - BLOCK + jnp.arange(3 * BLOCK)[None, :]
    rel = k_pos[:, None, :] - q_pos[:, :, None]
    valid = (jnp.abs(rel) <= WINDOW) & (k_pos[:, None, :] >= 0) & (k_pos[:, None, :] < L)
    s_loc = jnp.where(valid[None, :, None, None], s_loc, -jnp.inf)
    p = softmax_with_sink(jnp.concatenate([s_loc, s_ctx], axis=-1), sink[None, None, :, :, None, None])
    p = p.astype(v.dtype)
    p_loc, p_ctx = p[..., :3 * BLOCK], p[..., 3 * BLOCK:]
    o = (jnp.einsum('bnhgqj,bnjhd->bnqhgd', p_loc, v_band)
         + jnp.einsum('bnhgqc,bchd->bnqhgd', p_ctx, v_ctx))
    return o.reshape(B, L, Hk * G * v.shape[-1])


def s5_discretize(a_re, a_im, log_dt, b_re, b_im):
    a_re = a_re.astype(jnp.float32)
    a_im = a_im.astype(jnp.float32)
    dt = jnp.exp(log_dt.astype(jnp.float32))[:, None]
    mag = jnp.exp(a_re * dt)
    ab_re = mag * jnp.cos(a_im * dt)
    ab_im = mag * jnp.sin(a_im * dt)
    den = a_re * a_re + a_im * a_im
    n_re = ab_re - 1.0
    k_re = (n_re * a_re + ab_im * a_im) / den
    k_im = (ab_im * a_re - n_re * a_im) / den
    b_re = b_re.astype(jnp.float32)
    b_im = b_im.astype(jnp.float32)
    bb_re = k_re[..., None] * b_re - k_im[..., None] * b_im
    bb_im = k_re[..., None] * b_im + k_im[..., None] * b_re
    return ab_re, ab_im, bb_re, bb_im


def complex_scan_op(e1, e2):
    a1r, a1i, b1r, b1i = e1
    a2r, a2i, b2r, b2i = e2
    return (a2r * a1r - a2i * a1i, a2r * a1i + a2i * a1r,
            a2r * b1r - a2i * b1i + b2r, a2r * b1i + a2i * b1r + b2i)


def s5_scan(u, ab_re, ab_im, bb_re, bb_im, h0_re, h0_im, reverse):
    if reverse:
        u = jnp.flip(u, axis=1)
    bu_re = jnp.einsum('gph,blgh->blgp', bb_re, u)
    bu_im = jnp.einsum('gph,blgh->blgp', bb_im, u)
    bu_re = bu_re.at[:, 0].add(ab_re * h0_re - ab_im * h0_im)
    bu_im = bu_im.at[:, 0].add(ab_re * h0_im + ab_im * h0_re)
    a_re = jnp.broadcast_to(ab_re, bu_re.shape)
    a_im = jnp.broadcast_to(ab_im, bu_im.shape)
    _, _, h_re, h_im = lax.associative_scan(complex_scan_op, (a_re, a_im, bu_re, bu_im), axis=1)
    fin_re, fin_im = h_re[:, -1], h_im[:, -1]
    if reverse:
        h_re, h_im = jnp.flip(h_re, axis=1), jnp.flip(h_im, axis=1)
    return h_re, h_im, fin_re, fin_im


def ssm_mixer(u, lp, h0_re, h0_im):
    B, L, _ = u.shape
    uf = u.astype(jnp.float32).reshape(B, L, SSM_GROUPS, SSM_CH)
    y = lp['ssm_d'].astype(jnp.float32) * uf
    fins_re, fins_im = [], []
    for dirn in range(2):
        ab_re, ab_im, bb_re, bb_im = s5_discretize(lp['ssm_a_re'][dirn], lp['ssm_a_im'][dirn], lp['ssm_log_dt'][dirn],
                                                   lp['ssm_b_re'][dirn], lp['ssm_b_im'][dirn])
        h_re, h_im, f_re, f_im = s5_scan(uf, ab_re, ab_im, bb_re, bb_im,
                                         h0_re[:, dirn].astype(jnp.float32), h0_im[:, dirn].astype(jnp.float32),
                                         reverse=(dirn == 1))
        c_re = lp['ssm_c_re'][dirn].astype(jnp.float32)
        c_im = lp['ssm_c_im'][dirn].astype(jnp.float32)
        y = y + jnp.einsum('ghp,blgp->blgh', c_re, h_re) - jnp.einsum('ghp,blgp->blgh', c_im, h_im)
        fins_re.append(f_re)
        fins_im.append(f_im)
    g = jax.nn.gelu(y.reshape(B, L, SSM_WIDTH))
    out = g * jax.nn.sigmoid(g @ lp['w_glu'].astype(jnp.float32))
    return out.astype(u.dtype), jnp.stack(fins_re, axis=1), jnp.stack(fins_im, axis=1)


def attn_a_project(qa, ka, va, lp):
    B, L, _ = qa.shape
    q = rms_norm(qa.reshape(B, L, A_HEADS, A_HEAD_DIM), lp['a_q_norm'])
    k = rms_norm(ka.reshape(B, L, A_KV_HEADS, A_HEAD_DIM), lp['a_k_norm'])
    v = va.reshape(B, L, A_KV_HEADS, A_HEAD_DIM)
    return q, k, v


def mla_query_latent(cq, ckv, lp):
    B, L, _ = cq.shape
    q = rms_norm(cq, lp['mla_q_norm']) @ lp['w_q_b']
    q = rms_norm(q.reshape(B, L, MLA_HEADS, MLA_QK_DIM), lp['mla_qh_norm'])
    return q, rms_norm(ckv, lp['mla_kv_norm'])


def mla_keys_values(ckv_n, kpe, lp):
    B, L, _ = ckv_n.shape
    kv = (ckv_n @ lp['w_kv_b']).reshape(B, L, MLA_HEADS, MLA_NOPE + MLA_V)
    k_nope, v = kv[..., :MLA_NOPE], kv[..., MLA_NOPE:]
    k_pe = jnp.broadcast_to(kpe[:, :, None, :], (B, L, MLA_HEADS, MLA_ROPE)).astype(k_nope.dtype)
    k = rms_norm(jnp.concatenate([k_nope, k_pe], axis=-1), lp['mla_kh_norm'])
    return k, v


def rope_tail(x, row_pos, col_pos):
    return jnp.concatenate([x[..., :MLA_NOPE], rope_2d(x[..., MLA_NOPE:], row_pos, col_pos)], axis=-1)


def split_projection(h, w_in_l):
    offsets = np.cumsum(IN_SIZES)[:-1].tolist()
    return jnp.split(h @ w_in_l, offsets, axis=-1)


def adaln(cvec, w_ada_l, b_ada_l):
    m = jax.nn.silu(cvec) @ w_ada_l + b_ada_l
    return jnp.split(m, 6, axis=-1)


def modulate(x, g, shift, scale):
    return rms_norm(x, g) * (1.0 + scale) + shift


def moe(h, w_router, router_bias, w_gate, w_up, w_down):
    B, L, D = h.shape
    t = h.reshape(B * L, D)
    scores = jax.nn.sigmoid((t @ w_router).astype(jnp.float32))
    biased = scores + router_bias.astype(jnp.float32)
    grouped = biased.reshape(-1, N_EXPERT_GROUPS, EXPERTS_PER_GROUP)
    group_score = lax.top_k(grouped, GROUP_SCORE_K)[0].sum(-1)
    best_group = jnp.argmax(group_score, axis=-1)
    in_group = (jnp.arange(N_EXPERTS) // EXPERTS_PER_GROUP)[None, :] == best_group[:, None]
    _, idx = lax.top_k(jnp.where(in_group, biased, -jnp.inf), TOP_K)
    w = jnp.take_along_axis(scores, idx, axis=-1)
    w = w / jnp.sum(w, axis=-1, keepdims=True)
    combine = jnp.einsum('tk,tke->te', w, jax.nn.one_hot(idx, N_EXPERTS, dtype=jnp.float32))
    y = jnp.zeros((B * L, D), jnp.float32)
    for e in range(N_EXPERTS):
        act = jax.nn.silu(t @ w_gate[e]) * (t @ w_up[e])
        y = y + combine[:, e:e + 1] * (act @ w_down[e]).astype(jnp.float32)
    return y.astype(h.dtype).reshape(B, L, D)


def context_layer(x, mod, lp, w_router, router_bias):
    sh1, sc1, g1, sh2, sc2, g2 = mod
    B, S, _ = x.shape
    h = modulate(x, lp['norm_mix'], sh1, sc1)
    qa, ka, va, u, cq, ckv, kpe = split_projection(h, lp['w_in'])
    q, k, v = attn_a_project(qa, ka, va, lp)
    o_a = dense_attention(q.reshape(B, S, A_KV_HEADS, A_GROUP, A_HEAD_DIM), k, v,
                          lp['a_sink'].reshape(A_KV_HEADS, A_GROUP))
    zeros = jnp.zeros((B, 2, SSM_GROUPS, SSM_STATE), jnp.float32)
    o_b, s_re, s_im = ssm_mixer(u, lp, zeros, zeros)
    qc, ckv_n = mla_query_latent(cq, ckv, lp)
    kc, vc = mla_keys_values(ckv_n, kpe, lp)
    o_c = dense_attention(qc[:, :, :, None, :], kc, vc, None)
    x = x + g1 * (jnp.concatenate([o_a, o_b, o_c], axis=-1) @ lp['w_out'])
    h2 = modulate(x, lp['norm_ffn'], sh2, sc2)
    x = x + g2 * moe(h2, w_router, router_bias, lp['w_gate'], lp['w_up'], lp['w_down'])
    return x, (k, v, ckv_n, kpe, s_re, s_im)


def latent_layer(x, mod, lp, w_router, router_bias, ctx):
    ck, cv, cckv, ckpe, s0_re, s0_im = ctx
    sh1, sc1, g1, sh2, sc2, g2 = mod
    B, L, _ = x.shape
    row_pos, col_pos = grid_positions(L)
    h = modulate(x, lp['norm_mix'], sh1, sc1)
    qa, ka, va, u, cq, ckv, kpe = split_projection(h, lp['w_in'])
    q, k, v = attn_a_project(qa, ka, va, lp)
    q = rope_2d(q, row_pos, col_pos)
    k = rope_2d(k, row_pos, col_pos)
    o_a = window_attention(q.reshape(B, L, A_KV_HEADS, A_GROUP, A_HEAD_DIM), k, v, ck, cv,
                           lp['a_sink'].reshape(A_KV_HEADS, A_GROUP))
    o_b, _, _ = ssm_mixer(u, lp, s0_re, s0_im)
    qc, ckv_n = mla_query_latent(cq, ckv, lp)
    qc = rope_tail(qc, row_pos, col_pos)
    kl, vl = mla_keys_values(ckv_n, kpe, lp)
    kl = rope_tail(kl, row_pos, col_pos)
    kx, vx = mla_keys_values(cckv, ckpe, lp)
    o_c = dense_attention(qc[:, :, :, None, :], jnp.concatenate([kl, kx.astype(kl.dtype)], axis=1),
                          jnp.concatenate([vl, vx.astype(vl.dtype)], axis=1), None)
    x = x + g1 * (jnp.concatenate([o_a, o_b, o_c], axis=-1) @ lp['w_out'])
    h2 = modulate(x, lp['norm_ffn'], sh2, sc2)
    x = x + g2 * moe(h2, w_router, router_bias, lp['w_gate'], lp['w_up'], lp['w_down'])
    return x


def setup_inputs(seed: int = 0) -> dict:
    key = jax.random.key(seed)
    keys = iter(jax.random.split(key, 64))

    def normal(shape, scale=1.0):
        return scale * jax.random.normal(next(keys), shape, jnp.float32)

    def gain(shape):
        return 1.0 + 0.01 * jax.random.normal(next(keys), shape, jnp.float32)

    a_im_base = jnp.pi * jnp.arange(SSM_STATE, dtype=jnp.float32)
    return {
        'x_prompt': normal((BATCH, SEQ, D_MODEL)),
        'x_sample': normal((DEC_BATCH, DEC_SEQ, D_MODEL)),
        'cache_attn_k': normal((DEC_BATCH, DEPTH, PAST_LEN, A_KV_HEADS, A_HEAD_DIM)),
        'cache_attn_v': normal((DEC_BATCH, DEPTH, PAST_LEN, A_KV_HEADS, A_HEAD_DIM)),
        'cache_mla_ckv': normal((DEC_BATCH, DEPTH, PAST_LEN, MLA_KV_LORA)),
        'cache_mla_kpe': normal((DEC_BATCH, DEPTH, PAST_LEN, MLA_ROPE)),
        'state_ssm_re': normal((DEC_BATCH, DEPTH, 2, SSM_GROUPS, SSM_STATE), 0.5),
        'state_ssm_im': normal((DEC_BATCH, DEPTH, 2, SSM_GROUPS, SSM_STATE), 0.5),
        'c': normal((DEC_BATCH, D_MODEL)),
        'c_ctx': normal((D_MODEL,)),
        'norm_mix': gain((DEPTH, D_MODEL)),
        'norm_ffn': gain((DEPTH, D_MODEL)),
        'w_ada': normal((DEPTH, D_MODEL, 6 * D_MODEL), 0.5 * D_MODEL ** -0.5),
        'b_ada': normal((DEPTH, 6 * D_MODEL), 0.01),
        'w_in': normal((DEPTH, D_MODEL, N_IN), D_MODEL ** -0.5),
        'a_q_norm': gain((DEPTH, A_HEAD_DIM)),
        'a_k_norm': gain((DEPTH, A_HEAD_DIM)),
        'a_sink': normal((DEPTH, A_HEADS), 0.5),
        'ssm_a_re': -0.5 + normal((DEPTH, 2, SSM_GROUPS, SSM_STATE), 0.01),
        'ssm_a_im': a_im_base + normal((DEPTH, 2, SSM_GROUPS, SSM_STATE), 0.01),
        'ssm_log_dt': jax.random.uniform(next(keys), (DEPTH, 2, SSM_GROUPS), jnp.float32,
                                         math.log(1e-3), math.log(1e-1)),
        'ssm_b_re': normal((DEPTH, 2, SSM_GROUPS, SSM_STATE, SSM_CH), (2 * SSM_CH) ** -0.5),
        'ssm_b_im': normal((DEPTH, 2, SSM_GROUPS, SSM_STATE, SSM_CH), (2 * SSM_CH) ** -0.5),
        'ssm_c_re': normal((DEPTH, 2, SSM_GROUPS, SSM_CH, SSM_STATE), SSM_STATE ** -0.5),
        'ssm_c_im': normal((DEPTH, 2, SSM_GROUPS, SSM_CH, SSM_STATE), SSM_STATE ** -0.5),
        'ssm_d': normal((DEPTH, SSM_GROUPS, SSM_CH)),
        'w_glu': normal((DEPTH, SSM_WIDTH, SSM_WIDTH), SSM_WIDTH ** -0.5),
        'mla_q_norm': gain((DEPTH, MLA_Q_LORA)),
        'mla_kv_norm': gain((DEPTH, MLA_KV_LORA)),
        'w_q_b': normal((DEPTH, MLA_Q_LORA, MLA_HEADS * MLA_QK_DIM), MLA_Q_LORA ** -0.5),
        'w_kv_b': normal((DEPTH, MLA_KV_LORA, MLA_HEADS * (MLA_NOPE + MLA_V)), MLA_KV_LORA ** -0.5),
        'mla_qh_norm': gain((DEPTH, MLA_QK_DIM)),
        'mla_kh_norm': gain((DEPTH, MLA_QK_DIM)),
        'w_out': normal((DEPTH, D_MIX, D_MODEL), D_MIX ** -0.5),
        'w_router': normal((D_MODEL, N_EXPERTS), D_MODEL ** -0.5),
        'router_bias': normal((N_EXPERTS,), 0.01),
        'w_gate': normal((DEPTH, N_EXPERTS, D_MODEL, D_EXPERT), D_MODEL ** -0.5),
        'w_up': normal((DEPTH, N_EXPERTS, D_MODEL, D_EXPERT), D_MODEL ** -0.5),
        'w_down': normal((DEPTH, N_EXPERTS, D_EXPERT, D_MODEL), D_EXPERT ** -0.5),
    }


def reference(x_prompt, x_sample, cache_attn_k, cache_attn_v, cache_mla_ckv, cache_mla_kpe,
              state_ssm_re, state_ssm_im, c, c_ctx, norm_mix, norm_ffn, w_ada, b_ada, w_in,
              a_q_norm, a_k_norm, a_sink, ssm_a_re, ssm_a_im, ssm_log_dt, ssm_b_re, ssm_b_im,
              ssm_c_re, ssm_c_im, ssm_d, w_glu, mla_q_norm, mla_kv_norm, w_q_b, w_kv_b,
              mla_qh_norm, mla_kh_norm, w_out, w_router, router_bias, w_gate, w_up, w_down):
    y_prompt = x_prompt
    y_sample = x_sample
    ks, vs, ckvs, kpes, sres, sims = [], [], [], [], [], []
    for l in range(DEPTH):
        lp = {
            'norm_mix': norm_mix[l], 'norm_ffn': norm_ffn[l], 'w_in': w_in[l],
            'a_q_norm': a_q_norm[l], 'a_k_norm': a_k_norm[l], 'a_sink': a_sink[l],
            'ssm_a_re': ssm_a_re[l], 'ssm_a_im': ssm_a_im[l], 'ssm_log_dt': ssm_log_dt[l],
            'ssm_b_re': ssm_b_re[l], 'ssm_b_im': ssm_b_im[l], 'ssm_c_re': ssm_c_re[l],
            'ssm_c_im': ssm_c_im[l], 'ssm_d': ssm_d[l], 'w_glu': w_glu[l],
            'mla_q_norm': mla_q_norm[l], 'mla_kv_norm': mla_kv_norm[l], 'w_q_b': w_q_b[l],
            'w_kv_b': w_kv_b[l], 'mla_qh_norm': mla_qh_norm[l], 'mla_kh_norm': mla_kh_norm[l],
            'w_out': w_out[l], 'w_gate': w_gate[l], 'w_up': w_up[l], 'w_down': w_down[l],
        }
        mod_ctx = adaln(c_ctx, w_ada[l], b_ada[l])
        mod_lat = [m[:, None, :] for m in adaln(c, w_ada[l], b_ada[l])]
        y_prompt, (k_l, v_l, ckv_l, kpe_l, sre_l, sim_l) = context_layer(y_prompt, mod_ctx, lp, w_router, router_bias)
        ks.append(k_l)
        vs.append(v_l)
        ckvs.append(ckv_l)
        kpes.append(kpe_l)
        sres.append(sre_l)
        sims.append(sim_l)
        ctx = (cache_attn_k[:, l], cache_attn_v[:, l], cache_mla_ckv[:, l], cache_mla_kpe[:, l],
               state_ssm_re[:, l], state_ssm_im[:, l])
        y_sample = latent_layer(y_sample, mod_lat, lp, w_router, router_bias, ctx)
    new_attn_k = jnp.stack(ks, axis=1)
    new_attn_v = jnp.stack(vs, axis=1)
    new_mla_ckv = jnp.stack(ckvs, axis=1)
    new_mla_kpe = jnp.stack(kpes, axis=1)
    new_ssm_re = jnp.stack(sres, axis=1)
    new_ssm_im = jnp.stack(sims, axis=1)
    return (y_prompt, y_sample, new_attn_k, new_attn_v, new_mla_ckv, new_mla_kpe, new_ssm_re, new_ssm_im)
```

```python
import functools
import math

import jax
import jax.numpy as jnp
from jax import lax
from jax.experimental import pallas as pl
from jax.experimental.pallas import tpu as pltpu

F32 = jnp.float32
BF16 = jnp.bfloat16

D = 1024
N_CTX_B, CTX_L = 32, 256
N_LAT_B, LAT_L = 4, 4096
DEPTH = 4
GRID_W = 64
EPS = 1e-6
ROPE_BASE = 10000.0
A_HEADS, A_KV, A_DH = 8, 2, 64
SSM_G, SSM_H, SSM_P = 16, 16, 64
SSM_W = SSM_G * SSM_H
SSM_N = SSM_G * SSM_P
SSM_SLABS = SSM_N // 128
C_HEADS, C_NOPE, C_ROPE, C_V = 4, 64, 32, 64
C_QK = C_NOPE + C_ROPE
C_QLORA, C_KVLORA = 256, 128
N_EXP, N_GRP, GRP_SZ, D_EXP = 16, 4, 4, 512
WINDOW = 128

T_CTX = N_CTX_B * CTX_L
T_LAT = N_LAT_B * LAT_L
T_ALL = T_CTX + T_LAT

LANES = 128
VMEM_LIMIT = 48 * 1024 * 1024

TM = 256
N_CTX_TILES = T_CTX // TM
LAT_TILES_PER_B = LAT_L // TM
TQ_A = 128
TQ_C = 256
TM_MOE = 512
SSM_CHUNK = 256
NEG = -1e30


def _dot(a, b):
    return jnp.dot(a, b, preferred_element_type=F32)


def _dot_nt(a, b):
    return lax.dot_general(a, b, (((1,), (1,)), ((), ())), preferred_element_type=F32)


def _split_bf16(x):
    hi = x.astype(BF16)
    lo = (x - hi.astype(F32)).astype(BF16)
    return hi, lo


def _params(sem):
    return pltpu.CompilerParams(dimension_semantics=sem, vmem_limit_bytes=VMEM_LIMIT)


def _full(shape):
    n = len(shape)
    return pl.BlockSpec(shape, lambda *_: (0,) * n)


def _mod_row(i):
    return jnp.where(i < N_CTX_TILES, 0, 1 + (i - N_CTX_TILES) // LAT_TILES_PER_B)


def _pos_block(i):
    return jnp.where(i < N_CTX_TILES, LAT_TILES_PER_B, (i - N_CTX_TILES) % LAT_TILES_PER_B)


def _adaln_kernel(c_ref, w_ref, b_ref, o_ref):
    c = c_ref[...]
    s = c * jax.nn.sigmoid(c)
    s_hi, s_lo = _split_bf16(s)
    w_hi, w_lo = _split_bf16(w_ref[...])
    o_ref[...] = _dot(s_hi, w_hi) + _dot(s_hi, w_lo) + _dot(s_lo, w_hi) + b_ref[...]


def _adaln(cond, w_ada, b_ada):
    nt = 6
    out = pl.pallas_call(
        _adaln_kernel,
        grid=(DEPTH, nt),
        in_specs=[
            pl.BlockSpec((16, D), lambda l, n: (0, 0)),
            pl.BlockSpec((None, D, D), lambda l, n: (l, 0, n)),
            pl.BlockSpec((None, 1, D), lambda l, n: (l, 0, n)),
        ],
        out_specs=pl.BlockSpec((None, 16, D), lambda l, n: (l, 0, n)),
        out_shape=jax.ShapeDtypeStruct((DEPTH, 16, nt * D), F32),
        compiler_params=_params(("parallel", "parallel")),
        name="adaln",
    )(cond, w_ada, b_ada.reshape(DEPTH, 1, nt * D))
    return out.reshape(DEPTH, 16, nt, D)


def _disc_kernel(are_ref, aim_ref, ldt_ref, bre_ref, bim_ref, abre_ref, abim_ref, bbre_ref, bbim_ref):
    a_re = are_ref[...]
    a_im = aim_ref[...]
    dt = jnp.exp(ldt_ref[...])
    mag = jnp.exp(a_re * dt)
    ab_re = mag * jnp.cos(a_im * dt)
    ab_im = mag * jnp.sin(a_im * dt)
    den = a_re * a_re + a_im * a_im
    n_re = ab_re - 1.0
    k_re = (n_re * a_re + ab_im * a_im) / den
    k_im = (ab_im * a_re - n_re * a_im) / den
    abre_ref[...] = ab_re
    abim_ref[...] = ab_im
    b_re = bre_ref[...]
    b_im = bim_ref[...]
    bbre_ref[...] = k_re * b_re - k_im * b_im
    bbim_ref[...] = k_re * b_im + k_im * b_re


def _discretize(ssm_a_re, ssm_a_im, ssm_log_dt, ssm_b_re, ssm_b_im):
    n = DEPTH * 2 * SSM_G
    a_re = ssm_a_re.reshape(n, 1, SSM_P)
    a_im = ssm_a_im.reshape(n, 1, SSM_P)
    ldt = jnp.broadcast_to(ssm_log_dt.reshape(n, 1, 1), (n, 1, SSM_P))
    b_re = jnp.swapaxes(ssm_b_re.reshape(n, SSM_P, SSM_H), 1, 2)
    b_im = jnp.swapaxes(ssm_b_im.reshape(n, SSM_P, SSM_H), 1, 2)
    sa = jax.ShapeDtypeStruct((n, 1, SSM_P), F32)
    sb = jax.ShapeDtypeStruct((n, SSM_H, SSM_P), F32)
    return pl.pallas_call(
        _disc_kernel,
        out_shape=(sa, sa, sb, sb),
        compiler_params=pltpu.CompilerParams(vmem_limit_bytes=VMEM_LIMIT),
        name="s5_discretize",
    )(a_re, a_im, ldt, b_re, b_im)


def _rope_tables():
    t = jnp.arange(LAT_L)
    row = (t // GRID_W).astype(F32)[:, None]
    col = (t % GRID_W).astype(F32)[:, None]
    lane = jnp.arange(LANES)

    def build(d, width, valid):
        half = width // 2
        q = half // 2
        on_col = d >= half
        dd = jnp.where(on_col, d - half, d)
        first = dd < q
        j = jnp.where(first, dd, dd - q).astype(F32)
        inv = ROPE_BASE ** (-j / q)
        ang = jnp.where(on_col[None, :], col, row) * inv[None, :]
        cos = jnp.where(valid[None, :], jnp.cos(ang), 1.0)
        sin = jnp.where(valid[None, :], jnp.sin(ang), 0.0)
        s_up = jnp.where(first[None, :], -sin, 0.0)
        s_dn = jnp.where(first[None, :], 0.0, sin)
        tab = jnp.concatenate([cos, s_up, s_dn], axis=1)
        ident = jnp.concatenate([jnp.ones((TM, LANES), F32), jnp.zeros((TM, 2 * LANES), F32)], axis=1)
        return jnp.concatenate([tab, ident], axis=0)

    tab_a = build(lane % A_DH, A_DH, jnp.ones((LANES,), bool))
    dc = jnp.clip(lane - C_NOPE, 0, C_ROPE - 1)
    tab_c = build(dc, C_ROPE, (lane >= C_NOPE) & (lane < C_QK))
    return tab_a, tab_c


def _rope(x, tab_ref, shift):
    cos = tab_ref[:, 0:LANES]
    s_up = tab_ref[:, LANES:2 * LANES]
    s_dn = tab_ref[:, 2 * LANES:3 * LANES]
    return x * cos + pltpu.roll(x, LANES - shift, 1) * s_up + pltpu.roll(x, shift, 1) * s_dn


def _half_norm(xs, gain):
    lo = lax.broadcasted_iota(jnp.int32, xs.shape, 1) < A_DH
    sq = xs * xs
    s_lo = jnp.sum(jnp.where(lo, sq, 0.0), axis=-1, keepdims=True)
    s_hi = jnp.sum(jnp.where(lo, 0.0, sq), axis=-1, keepdims=True)
    r = jnp.where(lo, lax.rsqrt(s_lo * (1.0 / A_DH) + EPS), lax.rsqrt(s_hi * (1.0 / A_DH) + EPS))
    return xs * r * gain


def _mla_keys_values(ckv_n, kpe_slab, wkn_ref, wv_ref, gkh, tab_ref):
    cb = ckv_n.astype(BF16)
    kn = _dot(cb, wkn_ref[...])
    v = _dot(cb, wv_ref[...])
    ks = []
    for h in range(C_HEADS):
        slab = kn[:, h * LANES:(h + 1) * LANES] + kpe_slab
        ss = jnp.sum(slab * slab, axis=-1, keepdims=True)
        slab = slab * lax.rsqrt(ss * (1.0 / C_QK) + EPS) * gkh
        if tab_ref is not None:
            slab = _rope(slab, tab_ref, C_ROPE // 4)
        ks.append(slab.astype(BF16))
    return ks, v


def _proj_kernel(x_ref, mod_ref, gmix_ref, wq_ref, wkv_ref, wu_ref, wcq_ref, wt_ref,
                 gq_ref, gk_ref, gcq_ref, gckv_ref, wqb_ref, wkn_ref, wv_ref, gqh_ref, gkh_ref,
                 ta_ref, tc_ref,
                 qa_ref, ka_ref, va_ref, u_ref, qc_ref, kc_ref, vc_ref, ckv_ref, kpe_ref):
    x = x_ref[...]
    sh = mod_ref[0:1, :]
    sc = mod_ref[1:2, :]
    ms = jnp.mean(x * x, axis=-1, keepdims=True)
    h = x * lax.rsqrt(ms + EPS) * gmix_ref[...]
    h = (h * (1.0 + sc) + sh).astype(BF16)

    lo = lax.broadcasted_iota(jnp.int32, (TM, LANES), 1) < A_DH

    p_q = _dot(h, wq_ref[...])
    for j in range(A_HEADS // 2):
        slab = _half_norm(p_q[:, j * LANES:(j + 1) * LANES], gq_ref[...])
        slab = _rope(slab, ta_ref, A_DH // 4) * (A_DH ** -0.5)
        swapped = pltpu.roll(slab, A_DH, 1)
        kv_head = (2 * j) // (A_HEADS // A_KV)
        if kv_head == 0:
            q0 = jnp.where(lo, slab, 0.0)
            q1 = jnp.where(lo, swapped, 0.0)
        else:
            q0 = jnp.where(lo, 0.0, swapped)
            q1 = jnp.where(lo, 0.0, slab)
        qa_ref[:, (2 * j) * LANES:(2 * j + 1) * LANES] = q0.astype(BF16)
        qa_ref[:, (2 * j + 1) * LANES:(2 * j + 2) * LANES] = q1.astype(BF16)

    p_kv = _dot(h, wkv_ref[...])
    k = _half_norm(p_kv[:, 0:LANES], gk_ref[...])
    ka_ref[...] = _rope(k, ta_ref, A_DH // 4)
    va_ref[...] = p_kv[:, LANES:2 * LANES]

    u_ref[...] = _dot(h, wu_ref[...])

    p_cq = _dot(h, wcq_ref[...])
    ms = jnp.mean(p_cq * p_cq, axis=-1, keepdims=True)
    cq = (p_cq * lax.rsqrt(ms + EPS) * gcq_ref[...]).astype(BF16)
    q = _dot(cq, wqb_ref[...])
    for hd in range(C_HEADS):
        slab = q[:, hd * LANES:(hd + 1) * LANES]
        ss = jnp.sum(slab * slab, axis=-1, keepdims=True)
        slab = slab * lax.rsqrt(ss * (1.0 / C_QK) + EPS) * gqh_ref[...]
        slab = _rope(slab, tc_ref, C_ROPE // 4) * (C_QK ** -0.5)
        qc_ref[:, hd * LANES:(hd + 1) * LANES] = slab.astype(BF16)

    p_t = _dot(h, wt_ref[...])
    ckv = p_t[:, 0:LANES]
    ms = jnp.mean(ckv * ckv, axis=-1, keepdims=True)
    ckv_n = ckv * lax.rsqrt(ms + EPS) * gckv_ref[...]
    ckv_ref[...] = ckv_n
    kpe_wide = p_t[:, LANES:2 * LANES]
    kpe_ref[...] = kpe_wide[:, 0:C_ROPE]
    kpe_slab = pltpu.roll(kpe_wide, C_NOPE, 1)
    ks, v = _mla_keys_values(ckv_n, kpe_slab, wkn_ref, wv_ref, gkh_ref[...], tc_ref)
    for hd in range(C_HEADS):
        kc_ref[:, hd * LANES:(hd + 1) * LANES] = ks[hd]
    vc_ref[...] = v.astype(BF16)


def _proj(x, mod_l, w, tab_a, tab_c):
    nt = T_ALL // TM
    tile = lambda width: pl.BlockSpec((TM, width), lambda i: (i, 0))
    tab = pl.BlockSpec((TM, 3 * LANES), lambda i: (_pos_block(i), 0))
    in_specs = [
        tile(D),
        pl.BlockSpec((None, 6, D), lambda i: (_mod_row(i), 0, 0)),
        _full((1, D)),
        _full((D, 512)), _full((D, 256)), _full((D, 256)), _full((D, 256)), _full((D, 256)),
        _full((1, LANES)), _full((1, LANES)), _full((1, 256)), _full((1, LANES)),
        _full((256, 512)), _full((LANES, 512)), _full((LANES, 256)),
        _full((1, LANES)), _full((1, LANES)),
        tab, tab,
    ]
    outs = [(1024, BF16), (128, F32), (128, F32), (256, F32), (512, BF16), (512, BF16), (256, BF16),
            (128, F32), (C_ROPE, F32)]
    return pl.pallas_call(
        _proj_kernel,
        grid=(nt,),
        in_specs=in_specs,
        out_specs=[tile(wd) for wd, _ in outs],
        out_shape=[jax.ShapeDtypeStruct((T_ALL, wd), dt) for wd, dt in outs],
        compiler_params=_params(("parallel",)),
        name="proj",
    )(x, mod_l, w["gmix"], w["wq"], w["wkv"], w["wu"], w["wcq"], w["wt"],
      w["gq"], w["gk"], w["gcq"], w["gckv"], w["wqb"], w["wkn"], w["wv"], w["gqh"], w["gkh"],
      tab_a, tab_c)


def _cache_kv_kernel(ckv_ref, kpe_ref, wkn_ref, wv_ref, gkh_ref, k_ref, v_ref):
    ks, v = _mla_keys_values(ckv_ref[...], kpe_ref[...], wkn_ref, wv_ref, gkh_ref[...], None)
    for hd in range(C_HEADS):
        k_ref[:, hd * LANES:(hd + 1) * LANES] = ks[hd]
    v_ref[...] = v.astype(BF16)


def _cache_kv(cache_ckv, cache_kpe_slab, wkn, wv, gkh):
    return pl.pallas_call(
        _cache_kv_kernel,
        grid=(DEPTH, N_LAT_B),
        in_specs=[
            pl.BlockSpec((None, None, CTX_L, LANES), lambda l, b: (b, l, 0, 0)),
            pl.BlockSpec((None, None, CTX_L, LANES), lambda l, b: (b, l, 0, 0)),
            pl.BlockSpec((None, LANES, 512), lambda l, b: (l, 0, 0)),
            pl.BlockSpec((None, LANES, 256), lambda l, b: (l, 0, 0)),
            pl.BlockSpec((None, 1, LANES), lambda l, b: (l, 0, 0)),
        ],
        out_specs=[
            pl.BlockSpec((None, None, CTX_L, 512), lambda l, b: (l, b, 0, 0)),
            pl.BlockSpec((None, None, CTX_L, 256), lambda l, b: (l, b, 0, 0)),
        ],
        out_shape=[jax.ShapeDtypeStruct((DEPTH, N_LAT_B, CTX_L, 512), BF16),
                   jax.ShapeDtypeStruct((DEPTH, N_LAT_B, CTX_L, 256), BF16)],
        compiler_params=_params(("parallel", "parallel")),
        name="mla_cache_kv",
    )(cache_ckv, cache_kpe_slab, wkn, wv, gkh)


def _assemble_pairs(outs, o_ref, rows):
    lo = lax.broadcasted_iota(jnp.int32, (rows, LANES), 1) < A_DH
    for j in range(A_HEADS // 2):
        o0, o1 = outs[2 * j], outs[2 * j + 1]
        if (2 * j) // (A_HEADS // A_KV) == 0:
            slab = jnp.where(lo, o0, pltpu.roll(o1, A_DH, 1))
        else:
            slab = jnp.where(lo, pltpu.roll(o0, A_DH, 1), o1)
        o_ref[:, j * LANES:(j + 1) * LANES] = slab.astype(o_ref.dtype)


def _sink_column(sink_ref, kv_head, rows):
    grp = A_HEADS // A_KV
    return jnp.concatenate(
        [jnp.broadcast_to(sink_ref[kv_head * grp + g:kv_head * grp + g + 1, 0:1], (rows, 1)) for g in range(grp)],
        axis=0)


def _attn_a_ctx_kernel(q_ref, k_ref, v_ref, sink_ref, o_ref):
    grp = A_HEADS // A_KV
    kb = k_ref[...].astype(BF16)
    vb = v_ref[...].astype(BF16)
    outs = []
    for kvh in range(A_KV):
        q = jnp.concatenate([q_ref[:, (kvh * grp + g) * LANES:(kvh * grp + g + 1) * LANES] for g in range(grp)],
                            axis=0)
        s = _dot_nt(q, kb)
        sk = _sink_column(sink_ref, kvh, CTX_L)
        m = jnp.maximum(jnp.max(s, axis=-1, keepdims=True), sk)
        p = jnp.exp(s - m)
        den = jnp.sum(p, axis=-1, keepdims=True) + jnp.exp(sk - m)
        o = _dot(p.astype(BF16), vb) / den
        outs.extend(o[g * CTX_L:(g + 1) * CTX_L] for g in range(grp))
    _assemble_pairs(outs, o_ref, CTX_L)


def _attn_a_ctx(qa, ka, va, sink):
    return pl.pallas_call(
        _attn_a_ctx_kernel,
        grid=(N_CTX_B,),
        in_specs=[
            pl.BlockSpec((CTX_L, 1024), lambda b: (b, 0)),
            pl.BlockSpec((CTX_L, LANES), lambda b: (b, 0)),
            pl.BlockSpec((CTX_L, LANES), lambda b: (b, 0)),
            _full((A_HEADS, LANES)),
        ],
        out_specs=pl.BlockSpec((CTX_L, 512), lambda b: (b, 0)),
        out_shape=jax.ShapeDtypeStruct((T_CTX, 512), BF16),
        compiler_params=_params(("parallel",)),
        name="attn_a_ctx",
    )(qa, ka, va, sink)


def _attn_a_lat_kernel(q_ref, kp_ref, kc_ref, kn_ref, vp_ref, vc_ref, vn_ref, kx_ref, vx_ref, sink_ref, o_ref):
    grp = A_HEADS // A_KV
    n = pl.program_id(1)
    nb = LAT_L // TQ_A
    kb = jnp.concatenate([kp_ref[...], kc_ref[...], kn_ref[...]], axis=0).astype(BF16)
    vb = jnp.concatenate([vp_ref[...], vc_ref[...], vn_ref[...]], axis=0).astype(BF16)
    kx = kx_ref[...].astype(BF16)
    vx = vx_ref[...].astype(BF16)
    rows = grp * TQ_A
    qi = lax.broadcasted_iota(jnp.int32, (rows, 3 * TQ_A), 0) % TQ_A
    kj = lax.broadcasted_iota(jnp.int32, (rows, 3 * TQ_A), 1)
    rel = kj - TQ_A - qi
    valid = (jnp.abs(rel) <= WINDOW)
    valid = valid & ((kj >= TQ_A) | (n > 0)) & ((kj < 2 * TQ_A) | (n < nb - 1))
    outs = []
    for kvh in range(A_KV):
        q = jnp.concatenate([q_ref[:, (kvh * grp + g) * LANES:(kvh * grp + g + 1) * LANES] for g in range(grp)],
                            axis=0)
        s_loc = jnp.where(valid, _dot_nt(q, kb), NEG)
        s_ctx = _dot_nt(q, kx)
        sk = _sink_column(sink_ref, kvh, TQ_A)
        m = jnp.maximum(jnp.maximum(jnp.max(s_loc, axis=-1, keepdims=True),
                                    jnp.max(s_ctx, axis=-1, keepdims=True)), sk)
        p_loc = jnp.exp(s_loc - m)
        p_ctx = jnp.exp(s_ctx - m)
        den = (jnp.sum(p_loc, axis=-1, keepdims=True) + jnp.sum(p_ctx, axis=-1, keepdims=True)
               + jnp.exp(sk - m))
        o = (_dot(p_loc.astype(BF16), vb) + _dot(p_ctx.astype(BF16), vx)) / den
        outs.extend(o[g * TQ_A:(g + 1) * TQ_A] for g in range(grp))
    _assemble_pairs(outs, o_ref, TQ_A)


def _attn_a_lat(qa, ka, va, cache_k_l, cache_v_l, sink):
    nb = LAT_L // TQ_A
    base = T_CTX // TQ_A

    def kv_spec(off):
        return pl.BlockSpec((TQ_A, LANES), lambda b, n: (base + b * nb + jnp.clip(n + off, 0, nb - 1), 0))

    cache = pl.BlockSpec((None, CTX_L, LANES), lambda b, n: (b, 0, 0))
    return pl.pallas_call(
        _attn_a_lat_kernel,
        grid=(N_LAT_B, nb),
        in_specs=[
            pl.BlockSpec((TQ_A, 1024), lambda b, n: (base + b * nb + n, 0)),
            kv_spec(-1), kv_spec(0), kv_spec(1), kv_spec(-1), kv_spec(0), kv_spec(1),
            cache, cache, _full((A_HEADS, LANES)),
        ],
        out_specs=pl.BlockSpec((TQ_A, 512), lambda b, n: (b * nb + n, 0)),
        out_shape=jax.ShapeDtypeStruct((T_LAT, 512), BF16),
        compiler_params=_params(("parallel", "parallel")),
        name="attn_a_lat",
    )(qa, ka, ka, ka, va, va, va, cache_k_l, cache_v_l, sink)


def _pair_select(outs, o_ref, rows):
    lo = lax.broadcasted_iota(jnp.int32, (rows, LANES), 1) < C_V
    for j in range(C_HEADS // 2):
        o_ref[:, j * LANES:(j + 1) * LANES] = jnp.where(lo, outs[2 * j], outs[2 * j + 1]).astype(o_ref.dtype)


def _mla_ctx_kernel(q_ref, k_ref, v_ref, o_ref):
    outs = []
    for h in range(C_HEADS):
        s = _dot_nt(q_ref[:, h * LANES:(h + 1) * LANES], k_ref[:, h * LANES:(h + 1) * LANES])
        m = jnp.max(s, axis=-1, keepdims=True)
        p = jnp.exp(s - m)
        den = jnp.sum(p, axis=-1, keepdims=True)
        vs = v_ref[:, (h // 2) * LANES:(h // 2 + 1) * LANES]
        outs.append(_dot(p.astype(BF16), vs) / den)
    _pair_select(outs, o_ref, CTX_L)


def _mla_ctx(qc, kc, vc):
    return pl.pallas_call(
        _mla_ctx_kernel,
        grid=(N_CTX_B,),
        in_specs=[
            pl.BlockSpec((CTX_L, 512), lambda b: (b, 0)),
            pl.BlockSpec((CTX_L, 512), lambda b: (b, 0)),
            pl.BlockSpec((CTX_L, 256), lambda b: (b, 0)),
        ],
        out_specs=pl.BlockSpec((CTX_L, 256), lambda b: (b, 0)),
        out_shape=jax.ShapeDtypeStruct((T_CTX, 256), BF16),
        compiler_params=_params(("parallel",)),
        name="mla_ctx",
    )(qc, kc, vc)


def _mla_lat_kernel(q_ref, k_ref, v_ref, kx_ref, vx_ref, o_ref):
    outs = []
    for h in range(C_HEADS):
        hs = slice(h * LANES, (h + 1) * LANES)
        vsl = slice((h // 2) * LANES, (h // 2 + 1) * LANES)
        q = q_ref[:, hs]
        s_lat = _dot_nt(q, k_ref[:, hs])
        s_ctx = _dot_nt(q, kx_ref[:, hs])
        m = jnp.maximum(jnp.max(s_lat, axis=-1, keepdims=True), jnp.max(s_ctx, axis=-1, keepdims=True))
        p_lat = jnp.exp(s_lat - m)
        p_ctx = jnp.exp(s_ctx - m)
        den = jnp.sum(p_lat, axis=-1, keepdims=True) + jnp.sum(p_ctx, axis=-1, keepdims=True)
        o = _dot(p_lat.astype(BF16), v_ref[:, vsl]) + _dot(p_ctx.astype(BF16), vx_ref[:, vsl])
        outs.append(o / den)
    _pair_select(outs, o_ref, TQ_C)


def _mla_lat(qc, kc, vc, kx_l, vx_l):
    nq = LAT_L // TQ_C
    qbase = T_CTX // TQ_C
    kbase = T_CTX // LAT_L
    return pl.pallas_call(
        _mla_lat_kernel,
        grid=(N_LAT_B, nq),
        in_specs=[
            pl.BlockSpec((TQ_C, 512), lambda b, n: (qbase + b * nq + n, 0)),
            pl.BlockSpec((LAT_L, 512), lambda b, n: (kbase + b, 0)),
            pl.BlockSpec((LAT_L, 256), lambda b, n: (kbase + b, 0)),
            pl.BlockSpec((None, CTX_L, 512), lambda b, n: (b, 0, 0)),
            pl.BlockSpec((None, CTX_L, 256), lambda b, n: (b, 0, 0)),
        ],
        out_specs=pl.BlockSpec((TQ_C, 256), lambda b, n: (b * nq + n, 0)),
        out_shape=jax.ShapeDtypeStruct((T_LAT, 256), BF16),
        compiler_params=_params(("parallel", "arbitrary")),
        name="mla_lat",
    )(qc, kc, vc, kx_l, vx_l)


def _scan_rows(s_ref, a_re, a_im, h_re, h_im, nseq, length, reverse):
    def body(i, carry):
        t = (length - 1 - i) if reverse else i
        out = []
        for j in range(SSM_SLABS):
            hr, hi = carry[j], carry[SSM_SLABS + j]
            ar = a_re[:, j * LANES:(j + 1) * LANES]
            ai = a_im[:, j * LANES:(j + 1) * LANES]
            nr = ar * hr - ai * hi + s_ref[j, pl.ds(t, nseq, stride=length), :]
            ni = ar * hi + ai * hr + s_ref[SSM_SLABS + j, pl.ds(t, nseq, stride=length), :]
            s_ref[j, pl.ds(t, nseq, stride=length), :] = nr
            s_ref[SSM_SLABS + j, pl.ds(t, nseq, stride=length), :] = ni
            out.append((nr, ni))
        return tuple(o[0] for o in out) + tuple(o[1] for o in out)

    init = tuple(h_re[:, j * LANES:(j + 1) * LANES] for j in range(SSM_SLABS)) + \
        tuple(h_im[:, j * LANES:(j + 1) * LANES] for j in range(SSM_SLABS))
    fin = lax.fori_loop(0, length, body, init)
    return jnp.concatenate(fin[:SSM_SLABS], axis=1), jnp.concatenate(fin[SSM_SLABS:], axis=1)


def _ssm_direction(u_bf, bmat_ref, cmat_ref, a_ref, d, h_re, h_im, s_ref, nseq, length, reverse, write_y, step):
    rows = nseq * length
    for r in range(0, rows, step):
        bu = _dot(u_bf[r:r + step], bmat_ref[d])
        for j in range(2 * SSM_SLABS):
            s_ref[j, r:r + step, :] = bu[:, j * LANES:(j + 1) * LANES]
    a_re = jnp.broadcast_to(a_ref[d, 0:1, :], (nseq, SSM_N))
    a_im = jnp.broadcast_to(a_ref[d, 1:2, :], (nseq, SSM_N))
    h_re, h_im = _scan_rows(s_ref, a_re, a_im, h_re, h_im, nseq, length, reverse)
    for r in range(0, rows, step):
        h_all = jnp.concatenate([s_ref[j, r:r + step, :] for j in range(2 * SSM_SLABS)], axis=1)
        write_y(r, _dot(h_all.astype(BF16), cmat_ref[d]))
    return h_re, h_im


CTX_SEQS = 8


def _ssm_ctx_kernel(u_ref, bmat_ref, cmat_ref, a_ref, yf_ref, yb_ref, fin_ref, s_ref):
    step = 512
    u_bf = u_ref[...].astype(BF16)
    zero = jnp.zeros((CTX_SEQS, SSM_N), F32)
    for d, y_ref in ((0, yf_ref), (1, yb_ref)):
        def write_y(r, val, y_ref=y_ref):
            y_ref[r:r + step, :] = val

        h_re, h_im = _ssm_direction(u_bf, bmat_ref, cmat_ref, a_ref, d, zero, zero, s_ref,
                                    CTX_SEQS, CTX_L, d == 1, write_y, step)
        fin_ref[d, :, 0:SSM_N] = h_re
        fin_ref[d, :, SSM_N:2 * SSM_N] = h_im


def _ssm_ctx(u_ctx, bmat, cmat, a_bar):
    rows = CTX_SEQS * CTX_L
    blk = pl.BlockSpec((rows, SSM_W), lambda i: (i, 0))
    y = jax.ShapeDtypeStruct((T_CTX, SSM_W), F32)
    return pl.pallas_call(
        _ssm_ctx_kernel,
        grid=(N_CTX_B // CTX_SEQS,),
        in_specs=[blk, _full((2, SSM_W, 2 * SSM_N)), _full((2, 2 * SSM_N, SSM_W)), _full((2, 2, SSM_N))],
        out_specs=[blk, blk, pl.BlockSpec((2, CTX_SEQS, 2 * SSM_N), lambda i: (0, i, 0))],
        out_shape=[y, y, jax.ShapeDtypeStruct((2, N_CTX_B, 2 * SSM_N), F32)],
        scratch_shapes=[pltpu.VMEM((2 * SSM_SLABS, rows, LANES), F32)],
        compiler_params=_params(("parallel",)),
        name="ssm_ctx",
    )(u_ctx, bmat, cmat, a_bar)


def _ssm_lat_kernel(uf_ref, ub_ref, bmat_ref, cmat_ref, a_ref, h0_ref, yf_ref, yb_ref, s_ref, carry_ref):
    @pl.when(pl.program_id(0) == 0)
    def _():
        carry_ref[...] = h0_ref[...]

    rows = N_LAT_B * SSM_CHUNK
    for d, u_ref, y_ref in ((0, uf_ref, yf_ref), (1, ub_ref, yb_ref)):
        def write_y(r, val, y_ref=y_ref):
            y_ref[r // SSM_CHUNK] = val

        u_bf = u_ref[...].reshape(rows, SSM_W).astype(BF16)
        h_re = carry_ref[d, :, 0:SSM_N]
        h_im = carry_ref[d, :, SSM_N:2 * SSM_N]
        h_re, h_im = _ssm_direction(u_bf, bmat_ref, cmat_ref, a_ref, d, h_re, h_im, s_ref,
                                    N_LAT_B, SSM_CHUNK, d == 1, write_y, SSM_CHUNK)
        carry_ref[d, :, 0:SSM_N] = h_re
        carry_ref[d, :, SSM_N:2 * SSM_N] = h_im


def _ssm_lat(u_lat, bmat, cmat, a_bar, h0):
    nc = LAT_L // SSM_CHUNK
    fwd = pl.BlockSpec((N_LAT_B, SSM_CHUNK, SSM_W), lambda c: (0, c, 0))
    bwd = pl.BlockSpec((N_LAT_B, SSM_CHUNK, SSM_W), lambda c: (0, nc - 1 - c, 0))
    y = jax.ShapeDtypeStruct((N_LAT_B, LAT_L, SSM_W), F32)
    return pl.pallas_call(
        _ssm_lat_kernel,
        grid=(nc,),
        in_specs=[fwd, bwd, _full((2, SSM_W, 2 * SSM_N)), _full((2, 2 * SSM_N, SSM_W)), _full((2, 2, SSM_N)),
                  _full((2, N_LAT_B, 2 * SSM_N))],
        out_specs=[fwd, bwd],
        out_shape=[y, y],
        scratch_shapes=[pltpu.VMEM((2 * SSM_SLABS, N_LAT_B * SSM_CHUNK, LANES), F32),
                        pltpu.VMEM((2, N_LAT_B, 2 * SSM_N), F32)],
        compiler_params=_params(("arbitrary",)),
        name="ssm_lat",
    )(u_lat, u_lat, bmat, cmat, a_bar, h0)


def _gelu_tanh(x):
    return 0.5 * x * (1.0 + jnp.tanh(math.sqrt(2.0 / math.pi) * (x + 0.044715 * (x * x * x))))


def _outproj_kernel(x_ref, oa_ref, u_ref, yf_ref, yb_ref, oc_ref, mod_ref, d_ref, wglu_ref,
                    woa_ref, wob_ref, woc_ref, gffn_ref, wrhi_ref, wrlo_ref, rb_ref,
                    x1_ref, h2_ref, comb_ref):
    y = d_ref[...] * u_ref[...] + yf_ref[...] + yb_ref[...]
    g = _gelu_tanh(y)
    ob = g * jax.nn.sigmoid(_dot(g.astype(BF16), wglu_ref[...]))
    mix = _dot(oa_ref[...], woa_ref[...]) + _dot(ob.astype(BF16), wob_ref[...]) + _dot(oc_ref[...], woc_ref[...])
    x1 = x_ref[...] + mod_ref[2:3, :] * mix
    x1_ref[...] = x1
    ms = jnp.mean(x1 * x1, axis=-1, keepdims=True)
    h2 = x1 * lax.rsqrt(ms + EPS) * gffn_ref[...]
    h2 = h2 * (1.0 + mod_ref[4:5, :]) + mod_ref[3:4, :]
    h2_hi, h2_lo = _split_bf16(h2)
    h2_ref[...] = h2_hi

    lt = _dot_nt(wrhi_ref[...], h2_hi) + _dot_nt(wrhi_ref[...], h2_lo) + _dot_nt(wrlo_ref[...], h2_hi)
    sc = [jax.nn.sigmoid(lt[e:e + 1, :]) for e in range(N_EXP)]
    bi = [sc[e] + rb_ref[e:e + 1, 0:1] for e in range(N_EXP)]
    gs = []
    for gi in range(N_GRP):
        v = bi[gi * GRP_SZ:(gi + 1) * GRP_SZ]
        best2 = None
        for i in range(GRP_SZ):
            for j in range(i + 1, GRP_SZ):
                pair = v[i] + v[j]
                best2 = pair if best2 is None else jnp.maximum(best2, pair)
        gs.append(best2)
    best_g = jnp.zeros((1, TM), jnp.int32)
    best_v = gs[0]
    for gi in range(1, N_GRP):
        upd = gs[gi] > best_v
        best_g = jnp.where(upd, gi, best_g)
        best_v = jnp.where(upd, gs[gi], best_v)
    wts = []
    for e in range(N_EXP):
        gi = e // GRP_SZ
        beaten = jnp.zeros((1, TM), F32)
        for j in range(gi * GRP_SZ, (gi + 1) * GRP_SZ):
            if j == e:
                continue
            if j < e:
                beaten = beaten + jnp.where(bi[j] >= bi[e], 1.0, 0.0)
            else:
                beaten = beaten + jnp.where(bi[j] > bi[e], 1.0, 0.0)
        keep = jnp.where(best_g == gi, jnp.where(beaten < 1.5, 1.0, 0.0), 0.0)
        wts.append(keep * sc[e])
    tot = wts[0]
    for e in range(1, N_EXP):
        tot = tot + wts[e]
    inv = 1.0 / tot
    comb_t = jnp.concatenate([w_ * inv for w_ in wts] + [jnp.zeros((LANES - N_EXP, TM), F32)], axis=0)
    comb_ref[...] = comb_t.T


def _outproj(x, oa, u, yf, yb, oc, mod_l, w, wr_hi, wr_lo, rb):
    nt = T_ALL // TM
    tile = lambda width: pl.BlockSpec((TM, width), lambda i: (i, 0))
    in_specs = [
        tile(D), tile(512), tile(256), tile(256), tile(256), tile(256),
        pl.BlockSpec((None, 6, D), lambda i: (_mod_row(i), 0, 0)),
        _full((1, SSM_W)), _full((SSM_W, SSM_W)),
        _full((512, D)), _full((256, D)), _full((256, D)), _full((1, D)),
        _full((LANES, D)), _full((LANES, D)), _full((N_EXP, LANES)),
    ]
    return pl.pallas_call(
        _outproj_kernel,
        grid=(nt,),
        in_specs=in_specs,
        out_specs=[tile(D), tile(D), tile(LANES)],
        out_shape=[jax.ShapeDtypeStruct((T_ALL, D), F32), jax.ShapeDtypeStruct((T_ALL, D), BF16),
                   jax.ShapeDtypeStruct((T_ALL, LANES), F32)],
        compiler_params=_params(("parallel",)),
        name="outproj_router",
    )(x, oa, u, yf, yb, oc, mod_l, w["ssm_d"], w["wglu"], w["woa"], w["wob"], w["woc"], w["gffn"],
      wr_hi, wr_lo, rb)


def _moe_kernel(h_ref, comb_ref, x1_ref, mod_ref, wg_ref, wu_ref, wd_ref, o_ref, acc_ref):
    e = pl.program_id(1)

    @pl.when(e == 0)
    def _():
        acc_ref[...] = jnp.zeros_like(acc_ref)

    h = h_ref[...]
    a = _dot(h, wg_ref[...])
    b = _dot(h, wu_ref[...])
    act = (a * jax.nn.sigmoid(a)) * b
    y = _dot(act.astype(BF16), wd_ref[...])
    lane = lax.broadcasted_iota(jnp.int32, (TM_MOE, LANES), 1)
    wcol = jnp.sum(jnp.where(lane == e, comb_ref[...], 0.0), axis=-1, keepdims=True)
    acc_ref[...] += wcol * y

    @pl.when(e == N_EXP - 1)
    def _():
        o_ref[...] = x1_ref[...] + mod_ref[5:6, :] * acc_ref[...]


def _moe(h2, comb, x1, mod_l, wg, wu, wd):
    nt = T_ALL // TM_MOE
    per = TM_MOE // TM
    tile = lambda width: pl.BlockSpec((TM_MOE, width), lambda i, e: (i, 0))
    return pl.pallas_call(
        _moe_kernel,
        grid=(nt, N_EXP),
        in_specs=[
            tile(D), tile(LANES), tile(D),
            pl.BlockSpec((None, 6, D), lambda i, e: (_mod_row(i * per), 0, 0)),
            pl.BlockSpec((None, D, D_EXP), lambda i, e: (e, 0, 0)),
            pl.BlockSpec((None, D, D_EXP), lambda i, e: (e, 0, 0)),
            pl.BlockSpec((None, D_EXP, D), lambda i, e: (e, 0, 0)),
        ],
        out_specs=tile(D),
        out_shape=jax.ShapeDtypeStruct((T_ALL, D), F32),
        scratch_shapes=[pltpu.VMEM((TM_MOE, D), F32)],
        compiler_params=_params(("parallel", "arbitrary")),
        name="moe",
    )(h2, comb, x1, mod_l, wg, wu, wd)


def _pad_heads(w, heads, dim):
    k = w.shape[0]
    return jnp.pad(w.reshape(k, heads, dim), ((0, 0), (0, 0), (0, LANES - dim))).reshape(k, heads * LANES)


def _block_diag_b(bb_t):
    eye = jnp.eye(SSM_G, dtype=bb_t.dtype)
    return jnp.einsum("ghp,gk->ghkp", bb_t, eye).reshape(SSM_W, SSM_N)


def _block_diag_c(c):
    eye = jnp.eye(SSM_G, dtype=c.dtype)
    return jnp.einsum("ghp,gk->gpkh", c, eye).reshape(SSM_N, SSM_W)


def kernel(x_prompt, x_sample, cache_attn_k, cache_attn_v, cache_mla_ckv, cache_mla_kpe, state_ssm_re,
           state_ssm_im, c, c_ctx, norm_mix, norm_ffn, w_ada, b_ada, w_in, a_q_norm, a_k_norm, a_sink,
           ssm_a_re, ssm_a_im, ssm_log_dt, ssm_b_re, ssm_b_im, ssm_c_re, ssm_c_im, ssm_d, w_glu,
           mla_q_norm, mla_kv_norm, w_q_b, w_kv_b, mla_qh_norm, mla_kh_norm, w_out, w_router,
           router_bias, w_gate, w_up, w_down):
    x = jnp.concatenate([x_prompt.reshape(T_CTX, D), x_sample.reshape(T_LAT, D)], axis=0)

    cond = jnp.concatenate([c_ctx[None, :], c, jnp.zeros((16 - 1 - N_LAT_B, D), F32)], axis=0)
    mod = _adaln(cond, w_ada, b_ada)

    ab_re, ab_im, bbt_re, bbt_im = _discretize(ssm_a_re, ssm_a_im, ssm_log_dt, ssm_b_re, ssm_b_im)
    ab_re = ab_re.reshape(DEPTH, 2, SSM_N)
    ab_im = ab_im.reshape(DEPTH, 2, SSM_N)
    bbt_re = bbt_re.reshape(DEPTH, 2, SSM_G, SSM_H, SSM_P)
    bbt_im = bbt_im.reshape(DEPTH, 2, SSM_G, SSM_H, SSM_P)

    tab_a, tab_c = _rope_tables()

    wr_t = jnp.pad(w_router.T, ((0, LANES - N_EXP), (0, 0)))
    wr_hi = wr_t.astype(BF16)
    wr_lo = (wr_t - wr_hi.astype(F32)).astype(BF16)
    rb = jnp.broadcast_to(router_bias[:, None], (N_EXP, LANES))

    w_kv4 = w_kv_b.reshape(DEPTH, C_KVLORA, C_HEADS, C_NOPE + C_V)
    wkn_all = jnp.pad(w_kv4[..., :C_NOPE], ((0, 0), (0, 0), (0, 0), (0, LANES - C_NOPE))
                      ).reshape(DEPTH, C_KVLORA, C_HEADS * LANES).astype(BF16)
    wv_all = w_kv4[..., C_NOPE:].reshape(DEPTH, C_KVLORA, C_HEADS * C_V).astype(BF16)
    gkh_all = jnp.pad(mla_kh_norm, ((0, 0), (0, LANES - C_QK))).reshape(DEPTH, 1, LANES)

    kpe_slab = jnp.pad(cache_mla_kpe, ((0, 0), (0, 0), (0, 0), (C_NOPE, LANES - C_QK)))
    kx_all, vx_all = _cache_kv(cache_mla_ckv, kpe_slab, wkn_all, wv_all, gkh_all)

    cache_k = cache_attn_k.reshape(N_LAT_B, DEPTH, CTX_L, A_KV * A_DH)
    cache_v = cache_attn_v.reshape(N_LAT_B, DEPTH, CTX_L, A_KV * A_DH)

    ks, vs, ckvs, kpes, fins = [], [], [], [], []
    for l in range(DEPTH):
        wl = w_in[l]
        w = {
            "gmix": norm_mix[l][None, :],
            "wq": wl[:, 0:512].astype(BF16),
            "wkv": wl[:, 512:768].astype(BF16),
            "wu": wl[:, 768:1024].astype(BF16),
            "wcq": wl[:, 1024:1280].astype(BF16),
            "wt": jnp.pad(wl[:, 1280:1440], ((0, 0), (0, 256 - 160))).astype(BF16),
            "gq": jnp.tile(a_q_norm[l], 2)[None, :],
            "gk": jnp.tile(a_k_norm[l], 2)[None, :],
            "gcq": mla_q_norm[l][None, :],
            "gckv": mla_kv_norm[l][None, :],
            "wqb": _pad_heads(w_q_b[l], C_HEADS, C_QK).astype(BF16),
            "wkn": wkn_all[l],
            "wv": wv_all[l],
            "gqh": jnp.pad(mla_qh_norm[l], (0, LANES - C_QK))[None, :],
            "gkh": gkh_all[l],
            "ssm_d": ssm_d[l].reshape(1, SSM_W),
            "wglu": w_glu[l].astype(BF16),
            "woa": w_out[l, 0:512].astype(BF16),
            "wob": w_out[l, 512:768].astype(BF16),
            "woc": w_out[l, 768:1024].astype(BF16),
            "gffn": norm_ffn[l][None, :],
        }
        mod_l = mod[l]
        qa, ka, va, u, qc, kc, vc, ckv_n, kpe = _proj(x, mod_l, w, tab_a, tab_c)

        sink = jnp.broadcast_to(a_sink[l][:, None], (A_HEADS, LANES))
        oa_ctx = _attn_a_ctx(qa, ka, va, sink)
        oa_lat = _attn_a_lat(qa, ka, va, cache_k[:, l], cache_v[:, l], sink)
        oc_ctx = _mla_ctx(qc, kc, vc)
        oc_lat = _mla_lat(qc, kc, vc, kx_all[l], vx_all[l])

        bmat = jnp.stack([jnp.concatenate([_block_diag_b(bbt_re[l, d]), _block_diag_b(bbt_im[l, d])], axis=1)
                          for d in range(2)]).astype(BF16)
        cmat = jnp.stack([jnp.concatenate([_block_diag_c(ssm_c_re[l, d]), -_block_diag_c(ssm_c_im[l, d])], axis=0)
                          for d in range(2)]).astype(BF16)
        a_bar = jnp.stack([ab_re[l], ab_im[l]], axis=1)
        yf_ctx, yb_ctx, fin = _ssm_ctx(u[:T_CTX], bmat, cmat, a_bar)
        h0 = jnp.swapaxes(jnp.concatenate([state_ssm_re[:, l].reshape(N_LAT_B, 2, SSM_N),
                                            state_ssm_im[:, l].reshape(N_LAT_B, 2, SSM_N)], axis=-1), 0, 1)
        yf_lat, yb_lat = _ssm_lat(u[T_CTX:].reshape(N_LAT_B, LAT_L, SSM_W), bmat, cmat, a_bar, h0)

        oa = jnp.concatenate([oa_ctx, oa_lat], axis=0)
        oc = jnp.concatenate([oc_ctx, oc_lat], axis=0)
        yf = jnp.concatenate([yf_ctx, yf_lat.reshape(T_LAT, SSM_W)], axis=0)
        yb = jnp.concatenate([yb_ctx, yb_lat.reshape(T_LAT, SSM_W)], axis=0)
        x1, h2, comb = _outproj(x, oa, u, yf, yb, oc, mod_l, w, wr_hi, wr_lo, rb)
        x = _moe(h2, comb, x1, mod_l, w_gate[l].astype(BF16), w_up[l].astype(BF16), w_down[l].astype(BF16))

        ks.append(ka[:T_CTX].reshape(N_CTX_B, CTX_L, A_KV, A_DH))
        vs.append(va[:T_CTX].reshape(N_CTX_B, CTX_L, A_KV, A_DH))
        ckvs.append(ckv_n[:T_CTX].reshape(N_CTX_B, CTX_L, C_KVLORA))
        kpes.append(kpe[:T_CTX].reshape(N_CTX_B, CTX_L, C_ROPE))
        fins.append(fin)

    y_prompt = x[:T_CTX].reshape(N_CTX_B, CTX_L, D)
    y_sample = x[T_CTX:].reshape(N_LAT_B, LAT_L, D)
    fin = jnp.transpose(jnp.stack(fins, axis=0), (2, 0, 1, 3))
    new_re = fin[..., 0:SSM_N].reshape(N_CTX_B, DEPTH, 2, SSM_G, SSM_P)
    new_im = fin[..., SSM_N:].reshape(N_CTX_B, DEPTH, 2, SSM_G, SSM_P)
    return (y_prompt, y_sample, jnp.stack(ks, axis=1), jnp.stack(vs, axis=1), jnp.stack(ckvs, axis=1),
            jnp.stack(kpes, axis=1), new_re, new_im)
```

```python
import functools
import math

import jax
import jax.numpy as jnp
from jax import lax
from jax.experimental import pallas as pl
from jax.experimental.pallas import tpu as pltpu

F32 = jnp.float32
BF16 = jnp.bfloat16

D = 1024
N_CTX_B, CTX_L = 32, 256
N_LAT_B, LAT_L = 4, 4096
DEPTH = 4
GRID_W = 64
EPS = 1e-6
ROPE_BASE = 10000.0
A_HEADS, A_KV, A_DH = 8, 2, 64
SSM_G, SSM_H, SSM_P = 16, 16, 64
SSM_W = SSM_G * SSM_H
SSM_N = SSM_G * SSM_P
SSM_SLABS = SSM_N // 128
C_HEADS, C_NOPE, C_ROPE, C_V = 4, 64, 32, 64
C_QK = C_NOPE + C_ROPE
C_QLORA, C_KVLORA = 256, 128
N_EXP, N_GRP, GRP_SZ, D_EXP = 16, 4, 4, 512
WINDOW = 128

T_CTX = N_CTX_B * CTX_L
T_LAT = N_LAT_B * LAT_L
T_ALL = T_CTX + T_LAT

LANES = 128
VMEM_LIMIT = 48 * 1024 * 1024

TM = 256
N_LAT_TILES = T_LAT // TM
LAT_TILES_PER_B = LAT_L // TM
TQ_A = 128
TQ_C = 256
TM_MOE = 512
SSM_CHUNK = 256
SSM_GROUP = 4
NEG = -1e30


def _dot(a, b):
    return jnp.dot(a, b, preferred_element_type=F32)


def _dot_nt(a, b):
    return lax.dot_general(a, b, (((1,), (1,)), ((), ())), preferred_element_type=F32)


def _split_bf16(x):
    hi = x.astype(BF16)
    lo = (x - hi.astype(F32)).astype(BF16)
    return hi, lo


def _params(sem):
    return pltpu.CompilerParams(dimension_semantics=sem, vmem_limit_bytes=VMEM_LIMIT)


def _full(shape):
    n = len(shape)
    return pl.BlockSpec(shape, lambda *_: (0,) * n)


def _mod_row(i):
    return jnp.where(i < N_LAT_TILES, 1 + i // LAT_TILES_PER_B, 0)


def _pos_block(i):
    return jnp.where(i < N_LAT_TILES, i % LAT_TILES_PER_B, LAT_TILES_PER_B)


def _adaln_kernel(c_ref, w_ref, b_ref, o_ref):
    c = c_ref[...]
    s = c * jax.nn.sigmoid(c)
    s_hi, s_lo = _split_bf16(s)
    w_hi, w_lo = _split_bf16(w_ref[...])
    o_ref[...] = _dot(s_hi, w_hi) + _dot(s_hi, w_lo) + _dot(s_lo, w_hi) + b_ref[...]


def _adaln(cond, w_ada, b_ada):
    nt = 6
    out = pl.pallas_call(
        _adaln_kernel,
        grid=(DEPTH, nt),
        in_specs=[
            pl.BlockSpec((16, D), lambda l, n: (0, 0)),
            pl.BlockSpec((None, D, D), lambda l, n: (l, 0, n)),
            pl.BlockSpec((None, 1, D), lambda l, n: (l, 0, n)),
        ],
        out_specs=pl.BlockSpec((None, 16, D), lambda l, n: (l, 0, n)),
        out_shape=jax.ShapeDtypeStruct((DEPTH, 16, nt * D), F32),
        compiler_params=_params(("parallel", "parallel")),
        name="adaln",
    )(cond, w_ada, b_ada.reshape(DEPTH, 1, nt * D))
    return out.reshape(DEPTH, 16, nt, D)


def _disc_kernel(are_ref, aim_ref, ldt_ref, bre_ref, bim_ref, pwre_ref, pwim_ref, bbre_ref, bbim_ref):
    a_re = are_ref[...]
    a_im = aim_ref[...]
    dt = jnp.exp(ldt_ref[...])
    mag = jnp.exp(a_re * dt)
    ab_re = mag * jnp.cos(a_im * dt)
    ab_im = mag * jnp.sin(a_im * dt)
    den = a_re * a_re + a_im * a_im
    n_re = ab_re - 1.0
    k_re = (n_re * a_re + ab_im * a_im) / den
    k_im = (ab_im * a_re - n_re * a_im) / den
    b_re = bre_ref[...]
    b_im = bim_ref[...]
    bbre_ref[...] = k_re * b_re - k_im * b_im
    bbim_ref[...] = k_re * b_im + k_im * b_re
    p_re, p_im = ab_re, ab_im
    for k in range(8):
        pwre_ref[:, k:k + 1, :] = p_re
        pwim_ref[:, k:k + 1, :] = p_im
        p_re, p_im = p_re * ab_re - p_im * ab_im, p_re * ab_im + p_im * ab_re


def _discretize(ssm_a_re, ssm_a_im, ssm_log_dt, ssm_b_re, ssm_b_im):
    n = DEPTH * 2 * SSM_G
    a_re = ssm_a_re.reshape(n, 1, SSM_P)
    a_im = ssm_a_im.reshape(n, 1, SSM_P)
    ldt = jnp.broadcast_to(ssm_log_dt.reshape(n, 1, 1), (n, 1, SSM_P))
    b_re = jnp.swapaxes(ssm_b_re.reshape(n, SSM_P, SSM_H), 1, 2)
    b_im = jnp.swapaxes(ssm_b_im.reshape(n, SSM_P, SSM_H), 1, 2)
    sa = jax.ShapeDtypeStruct((n, 8, SSM_P), F32)
    sb = jax.ShapeDtypeStruct((n, SSM_H, SSM_P), F32)
    return pl.pallas_call(
        _disc_kernel,
        out_shape=(sa, sa, sb, sb),
        compiler_params=pltpu.CompilerParams(vmem_limit_bytes=VMEM_LIMIT),
        name="s5_discretize",
    )(a_re, a_im, ldt, b_re, b_im)


def _rope_tables():
    t = jnp.arange(LAT_L)
    row = (t // GRID_W).astype(F32)[:, None]
    col = (t % GRID_W).astype(F32)[:, None]
    lane = jnp.arange(LANES)

    def build(d, width, valid):
        half = width // 2
        q = half // 2
        on_col = d >= half
        dd = jnp.where(on_col, d - half, d)
        first = dd < q
        j = jnp.where(first, dd, dd - q).astype(F32)
        inv = ROPE_BASE ** (-j / q)
        ang = jnp.where(on_col[None, :], col, row) * inv[None, :]
        cos = jnp.where(valid[None, :], jnp.cos(ang), 1.0)
        sin = jnp.where(valid[None, :], jnp.sin(ang), 0.0)
        s_up = jnp.where(first[None, :], -sin, 0.0)
        s_dn = jnp.where(first[None, :], 0.0, sin)
        tab = jnp.concatenate([cos, s_up, s_dn], axis=1)
        ident = jnp.concatenate([jnp.ones((TM, LANES), F32), jnp.zeros((TM, 2 * LANES), F32)], axis=1)
        return jnp.concatenate([tab, ident], axis=0)

    tab_a = build(lane % A_DH, A_DH, jnp.ones((LANES,), bool))
    dc = jnp.clip(lane - C_NOPE, 0, C_ROPE - 1)
    tab_c = build(dc, C_ROPE, (lane >= C_NOPE) & (lane < C_QK))
    return tab_a, tab_c


def _rope(x, tab_ref, shift):
    cos = tab_ref[:, 0:LANES]
    s_up = tab_ref[:, LANES:2 * LANES]
    s_dn = tab_ref[:, 2 * LANES:3 * LANES]
    return x * cos + pltpu.roll(x, LANES - shift, 1) * s_up + pltpu.roll(x, shift, 1) * s_dn


def _half_norm(xs, gain):
    lo = lax.broadcasted_iota(jnp.int32, xs.shape, 1) < A_DH
    sq = xs * xs
    s_lo = jnp.sum(jnp.where(lo, sq, 0.0), axis=-1, keepdims=True)
    s_hi = jnp.sum(jnp.where(lo, 0.0, sq), axis=-1, keepdims=True)
    r = jnp.where(lo, lax.rsqrt(s_lo * (1.0 / A_DH) + EPS), lax.rsqrt(s_hi * (1.0 / A_DH) + EPS))
    return xs * r * gain


def _mla_keys_values(ckv_n, kpe_slab, wkn_ref, wv_ref, gkh, tab_ref):
    cb = ckv_n.astype(BF16)
    kn = _dot(cb, wkn_ref[...])
    v = _dot(cb, wv_ref[...])
    ks = []
    for h in range(C_HEADS):
        slab = kn[:, h * LANES:(h + 1) * LANES] + kpe_slab
        ss = jnp.sum(slab * slab, axis=-1, keepdims=True)
        slab = slab * lax.rsqrt(ss * (1.0 / C_QK) + EPS) * gkh
        if tab_ref is not None:
            slab = _rope(slab, tab_ref, C_ROPE // 4)
        ks.append(slab.astype(BF16))
    return ks, v


def _proj_kernel(x_ref, mod_ref, gmix_ref, wq_ref, wkv_ref, wu_ref, wcq_ref, wt_ref,
                 gq_ref, gk_ref, gcq_ref, gckv_ref, wqb_ref, wkn_ref, wv_ref, gqh_ref, gkh_ref,
                 ta_ref, tc_ref,
                 qa_ref, ka_ref, va_ref, u_ref, qc_ref, kc_ref, vc_ref, ckv_ref, kpe_ref):
    x = x_ref[...]
    sh = mod_ref[0:1, :]
    sc = mod_ref[1:2, :]
    ms = jnp.mean(x * x, axis=-1, keepdims=True)
    h = x * lax.rsqrt(ms + EPS) * gmix_ref[...]
    h = (h * (1.0 + sc) + sh).astype(BF16)

    lo = lax.broadcasted_iota(jnp.int32, (TM, LANES), 1) < A_DH

    p_q = _dot(h, wq_ref[...])
    for j in range(A_HEADS // 2):
        slab = _half_norm(p_q[:, j * LANES:(j + 1) * LANES], gq_ref[...])
        slab = _rope(slab, ta_ref, A_DH // 4) * (A_DH ** -0.5)
        swapped = pltpu.roll(slab, A_DH, 1)
        kv_head = (2 * j) // (A_HEADS // A_KV)
        if kv_head == 0:
            q0 = jnp.where(lo, slab, 0.0)
            q1 = jnp.where(lo, swapped, 0.0)
        else:
            q0 = jnp.where(lo, 0.0, swapped)
            q1 = jnp.where(lo, 0.0, slab)
        qa_ref[:, (2 * j) * LANES:(2 * j + 1) * LANES] = q0.astype(BF16)
        qa_ref[:, (2 * j + 1) * LANES:(2 * j + 2) * LANES] = q1.astype(BF16)

    p_kv = _dot(h, wkv_ref[...])
    k = _half_norm(p_kv[:, 0:LANES], gk_ref[...])
    ka_ref[...] = _rope(k, ta_ref, A_DH // 4)
    va_ref[...] = p_kv[:, LANES:2 * LANES]

    u_ref[...] = _dot(h, wu_ref[...])

    p_cq = _dot(h, wcq_ref[...])
    ms = jnp.mean(p_cq * p_cq, axis=-1, keepdims=True)
    cq = (p_cq * lax.rsqrt(ms + EPS) * gcq_ref[...]).astype(BF16)
    q = _dot(cq, wqb_ref[...])
    for hd in range(C_HEADS):
        slab = q[:, hd * LANES:(hd + 1) * LANES]
        ss = jnp.sum(slab * slab, axis=-1, keepdims=True)
        slab = slab * lax.rsqrt(ss * (1.0 / C_QK) + EPS) * gqh_ref[...]
        slab = _rope(slab, tc_ref, C_ROPE // 4) * (C_QK ** -0.5)
        qc_ref[:, hd * LANES:(hd + 1) * LANES] = slab.astype(BF16)

    p_t = _dot(h, wt_ref[...])
    ckv = p_t[:, 0:LANES]
    ms = jnp.mean(ckv * ckv, axis=-1, keepdims=True)
    ckv_n = ckv * lax.rsqrt(ms + EPS) * gckv_ref[...]
    ckv_ref[...] = ckv_n
    kpe_wide = p_t[:, LANES:2 * LANES]
    kpe_ref[...] = kpe_wide[:, 0:C_ROPE]
    kpe_slab = pltpu.roll(kpe_wide, C_NOPE, 1)
    ks, v = _mla_keys_values(ckv_n, kpe_slab, wkn_ref, wv_ref, gkh_ref[...], tc_ref)
    for hd in range(C_HEADS):
        kc_ref[:, hd * LANES:(hd + 1) * LANES] = ks[hd]
    vc_ref[...] = v.astype(BF16)


def _proj(x, mod_l, w, tab_a, tab_c):
    nt = T_ALL // TM
    tile = lambda width: pl.BlockSpec((TM, width), lambda i: (i, 0))
    tab = pl.BlockSpec((TM, 3 * LANES), lambda i: (_pos_block(i), 0))
    in_specs = [
        tile(D),
        pl.BlockSpec((None, 6, D), lambda i: (_mod_row(i), 0, 0)),
        _full((1, D)),
        _full((D, 512)), _full((D, 256)), _full((D, 256)), _full((D, 256)), _full((D, 256)),
        _full((1, LANES)), _full((1, LANES)), _full((1, 256)), _full((1, LANES)),
        _full((256, 512)), _full((LANES, 512)), _full((LANES, 256)),
        _full((1, LANES)), _full((1, LANES)),
        tab, tab,
    ]
    outs = [(1024, BF16), (128, F32), (128, F32), (256, F32), (512, BF16), (512, BF16), (256, BF16),
            (128, F32), (C_ROPE, F32)]
    return pl.pallas_call(
        _proj_kernel,
        grid=(nt,),
        in_specs=in_specs,
        out_specs=[tile(wd) for wd, _ in outs],
        out_shape=[jax.ShapeDtypeStruct((T_ALL, wd), dt) for wd, dt in outs],
        compiler_params=_params(("parallel",)),
        name="proj",
    )(x, mod_l, w["gmix"], w["wq"], w["wkv"], w["wu"], w["wcq"], w["wt"],
      w["gq"], w["gk"], w["gcq"], w["gckv"], w["wqb"], w["wkn"], w["wv"], w["gqh"], w["gkh"],
      tab_a, tab_c)


def _cache_kv_kernel(ckv_ref, kpe_ref, wkn_ref, wv_ref, gkh_ref, k_ref, v_ref):
    ks, v = _mla_keys_values(ckv_ref[...], kpe_ref[...], wkn_ref, wv_ref, gkh_ref[...], None)
    for hd in range(C_HEADS):
        k_ref[:, hd * LANES:(hd + 1) * LANES] = ks[hd]
    v_ref[...] = v.astype(BF16)


def _cache_kv(cache_ckv, cache_kpe_slab, wkn, wv, gkh):
    return pl.pallas_call(
        _cache_kv_kernel,
        grid=(DEPTH, N_LAT_B),
        in_specs=[
            pl.BlockSpec((None, None, CTX_L, LANES), lambda l, b: (b, l, 0, 0)),
            pl.BlockSpec((None, None, CTX_L, LANES), lambda l, b: (b, l, 0, 0)),
            pl.BlockSpec((None, LANES, 512), lambda l, b: (l, 0, 0)),
            pl.BlockSpec((None, LANES, 256), lambda l, b: (l, 0, 0)),
            pl.BlockSpec((None, 1, LANES), lambda l, b: (l, 0, 0)),
        ],
        out_specs=[
            pl.BlockSpec((None, None, CTX_L, 512), lambda l, b: (l, b, 0, 0)),
            pl.BlockSpec((None, None, CTX_L, 256), lambda l, b: (l, b, 0, 0)),
        ],
        out_shape=[jax.ShapeDtypeStruct((DEPTH, N_LAT_B, CTX_L, 512), BF16),
                   jax.ShapeDtypeStruct((DEPTH, N_LAT_B, CTX_L, 256), BF16)],
        compiler_params=_params(("parallel", "parallel")),
        name="mla_cache_kv",
    )(cache_ckv, cache_kpe_slab, wkn, wv, gkh)


def _assemble_pairs(outs, o_ref, rows):
    lo = lax.broadcasted_iota(jnp.int32, (rows, LANES), 1) < A_DH
    for j in range(A_HEADS // 2):
        o0, o1 = outs[2 * j], outs[2 * j + 1]
        if (2 * j) // (A_HEADS // A_KV) == 0:
            slab = jnp.where(lo, o0, pltpu.roll(o1, A_DH, 1))
        else:
            slab = jnp.where(lo, pltpu.roll(o0, A_DH, 1), o1)
        o_ref[:, j * LANES:(j + 1) * LANES] = slab.astype(o_ref.dtype)


def _sink_column(sink_ref, kv_head, rows):
    grp = A_HEADS // A_KV
    return jnp.concatenate(
        [jnp.broadcast_to(sink_ref[kv_head * grp + g:kv_head * grp + g + 1, 0:1], (rows, 1)) for g in range(grp)],
        axis=0)


def _attn_a_ctx_kernel(q_ref, k_ref, v_ref, sink_ref, o_ref):
    grp = A_HEADS // A_KV
    kb = k_ref[...].astype(BF16)
    vb = v_ref[...].astype(BF16)
    outs = []
    for kvh in range(A_KV):
        q = jnp.concatenate([q_ref[:, (kvh * grp + g) * LANES:(kvh * grp + g + 1) * LANES] for g in range(grp)],
                            axis=0)
        s = _dot_nt(q, kb)
        sk = _sink_column(sink_ref, kvh, CTX_L)
        m = jnp.maximum(jnp.max(s, axis=-1, keepdims=True), sk)
        p = jnp.exp(s - m)
        den = jnp.sum(p, axis=-1, keepdims=True) + jnp.exp(sk - m)
        o = _dot(p.astype(BF16), vb) / den
        outs.extend(o[g * CTX_L:(g + 1) * CTX_L] for g in range(grp))
    _assemble_pairs(outs, o_ref, CTX_L)


def _attn_a_ctx(qa, ka, va, sink):
    return pl.pallas_call(
        _attn_a_ctx_kernel,
        grid=(N_CTX_B,),
        in_specs=[
            pl.BlockSpec((CTX_L, 1024), lambda b: (N_LAT_TILES + b, 0)),
            pl.BlockSpec((CTX_L, LANES), lambda b: (N_LAT_TILES + b, 0)),
            pl.BlockSpec((CTX_L, LANES), lambda b: (N_LAT_TILES + b, 0)),
            _full((A_HEADS, LANES)),
        ],
        out_specs=pl.BlockSpec((CTX_L, 512), lambda b: (b, 0)),
        out_shape=jax.ShapeDtypeStruct((T_CTX, 512), BF16),
        compiler_params=_params(("parallel",)),
        name="attn_a_ctx",
    )(qa, ka, va, sink)


def _attn_a_lat_kernel(q_ref, kp_ref, kc_ref, kn_ref, vp_ref, vc_ref, vn_ref, kx_ref, vx_ref, sink_ref, o_ref):
    grp = A_HEADS // A_KV
    n = pl.program_id(1)
    nb = LAT_L // TQ_A
    kb = jnp.concatenate([kp_ref[...], kc_ref[...], kn_ref[...]], axis=0).astype(BF16)
    vb = jnp.concatenate([vp_ref[...], vc_ref[...], vn_ref[...]], axis=0).astype(BF16)
    kx = kx_ref[...].astype(BF16)
    vx = vx_ref[...].astype(BF16)
    rows = grp * TQ_A
    qi = lax.broadcasted_iota(jnp.int32, (rows, 3 * TQ_A), 0) % TQ_A
    kj = lax.broadcasted_iota(jnp.int32, (rows, 3 * TQ_A), 1)
    rel = kj - TQ_A - qi
    valid = (jnp.abs(rel) <= WINDOW)
    valid = valid & ((kj >= TQ_A) | (n > 0)) & ((kj < 2 * TQ_A) | (n < nb - 1))
    outs = []
    for kvh in range(A_KV):
        q = jnp.concatenate([q_ref[:, (kvh * grp + g) * LANES:(kvh * grp + g + 1) * LANES] for g in range(grp)],
                            axis=0)
        s_loc = jnp.where(valid, _dot_nt(q, kb), NEG)
        s_ctx = _dot_nt(q, kx)
        sk = _sink_column(sink_ref, kvh, TQ_A)
        m = jnp.maximum(jnp.maximum(jnp.max(s_loc, axis=-1, keepdims=True),
                                    jnp.max(s_ctx, axis=-1, keepdims=True)), sk)
        p_loc = jnp.exp(s_loc - m)
        p_ctx = jnp.exp(s_ctx - m)
        den = (jnp.sum(p_loc, axis=-1, keepdims=True) + jnp.sum(p_ctx, axis=-1, keepdims=True)
               + jnp.exp(sk - m))
        o = (_dot(p_loc.astype(BF16), vb) + _dot(p_ctx.astype(BF16), vx)) / den
        outs.extend(o[g * TQ_A:(g + 1) * TQ_A] for g in range(grp))
    _assemble_pairs(outs, o_ref, TQ_A)


def _attn_a_lat(qa, ka, va, cache_k_l, cache_v_l, sink):
    nb = LAT_L // TQ_A
    base = 0

    def kv_spec(off):
        return pl.BlockSpec((TQ_A, LANES), lambda b, n: (base + b * nb + jnp.clip(n + off, 0, nb - 1), 0))

    cache = pl.BlockSpec((None, CTX_L, LANES), lambda b, n: (b, 0, 0))
    return pl.pallas_call(
        _attn_a_lat_kernel,
        grid=(N_LAT_B, nb),
        in_specs=[
            pl.BlockSpec((TQ_A, 1024), lambda b, n: (base + b * nb + n, 0)),
            kv_spec(-1), kv_spec(0), kv_spec(1), kv_spec(-1), kv_spec(0), kv_spec(1),
            cache, cache, _full((A_HEADS, LANES)),
        ],
        out_specs=pl.BlockSpec((TQ_A, 512), lambda b, n: (b * nb + n, 0)),
        out_shape=jax.ShapeDtypeStruct((T_LAT, 512), BF16),
        compiler_params=_params(("parallel", "parallel")),
        name="attn_a_lat",
    )(qa, ka, ka, ka, va, va, va, cache_k_l, cache_v_l, sink)


def _pair_select(outs, o_ref, rows):
    lo = lax.broadcasted_iota(jnp.int32, (rows, LANES), 1) < C_V
    for j in range(C_HEADS // 2):
        o_ref[:, j * LANES:(j + 1) * LANES] = jnp.where(lo, outs[2 * j], outs[2 * j + 1]).astype(o_ref.dtype)


def _mla_ctx_kernel(q_ref, k_ref, v_ref, o_ref):
    outs = []
    for h in range(C_HEADS):
        s = _dot_nt(q_ref[:, h * LANES:(h + 1) * LANES], k_ref[:, h * LANES:(h + 1) * LANES])
        m = jnp.max(s, axis=-1, keepdims=True)
        p = jnp.exp(s - m)
        den = jnp.sum(p, axis=-1, keepdims=True)
        vs = v_ref[:, (h // 2) * LANES:(h // 2 + 1) * LANES]
        outs.append(_dot(p.astype(BF16), vs) / den)
    _pair_select(outs, o_ref, CTX_L)


def _mla_ctx(qc, kc, vc):
    return pl.pallas_call(
        _mla_ctx_kernel,
        grid=(N_CTX_B,),
        in_specs=[
            pl.BlockSpec((CTX_L, 512), lambda b: (N_LAT_TILES + b, 0)),
            pl.BlockSpec((CTX_L, 512), lambda b: (N_LAT_TILES + b, 0)),
            pl.BlockSpec((CTX_L, 256), lambda b: (N_LAT_TILES + b, 0)),
        ],
        out_specs=pl.BlockSpec((CTX_L, 256), lambda b: (b, 0)),
        out_shape=jax.ShapeDtypeStruct((T_CTX, 256), BF16),
        compiler_params=_params(("parallel",)),
        name="mla_ctx",
    )(qc, kc, vc)


def _mla_lat_kernel(q_ref, k_ref, v_ref, kx_ref, vx_ref, o_ref):
    outs = []
    for h in range(C_HEADS):
        hs = slice(h * LANES, (h + 1) * LANES)
        vsl = slice((h // 2) * LANES, (h // 2 + 1) * LANES)
        q = q_ref[:, hs]
        s_lat = _dot_nt(q, k_ref[:, hs])
        s_ctx = _dot_nt(q, kx_ref[:, hs])
        m = jnp.maximum(jnp.max(s_lat, axis=-1, keepdims=True), jnp.max(s_ctx, axis=-1, keepdims=True))
        p_lat = jnp.exp(s_lat - m)
        p_ctx = jnp.exp(s_ctx - m)
        den = jnp.sum(p_lat, axis=-1, keepdims=True) + jnp.sum(p_ctx, axis=-1, keepdims=True)
        o = _dot(p_lat.astype(BF16), v_ref[:, vsl]) + _dot(p_ctx.astype(BF16), vx_ref[:, vsl])
        outs.append(o / den)
    _pair_select(outs, o_ref, TQ_C)


def _mla_lat(qc, kc, vc, kx_l, vx_l):
    nq = LAT_L // TQ_C
    qbase = 0
    kbase = 0
    return pl.pallas_call(
        _mla_lat_kernel,
        grid=(N_LAT_B, nq),
        in_specs=[
            pl.BlockSpec((TQ_C, 512), lambda b, n: (qbase + b * nq + n, 0)),
            pl.BlockSpec((LAT_L, 512), lambda b, n: (kbase + b, 0)),
            pl.BlockSpec((LAT_L, 256), lambda b, n: (kbase + b, 0)),
            pl.BlockSpec((None, CTX_L, 512), lambda b, n: (b, 0, 0)),
            pl.BlockSpec((None, CTX_L, 256), lambda b, n: (b, 0, 0)),
        ],
        out_specs=pl.BlockSpec((TQ_C, 256), lambda b, n: (b * nq + n, 0)),
        out_shape=jax.ShapeDtypeStruct((T_LAT, 256), BF16),
        compiler_params=_params(("parallel", "arbitrary")),
        name="mla_lat",
    )(qc, kc, vc, kx_l, vx_l)


def _tile_scan(s_ref, hb_ref, tab_ref, d, row0, carry, reverse):
    n16 = SSM_CHUNK // 16

    def body(i, carry):
        blk = (n16 - 1 - i) if reverse else i
        r = pl.multiple_of(row0 + blk * 16, 16)
        halves = [None, None]
        for half in ((1, 0) if reverse else (0, 1)):
            x = s_ref[pl.ds(pl.multiple_of(r + half * 8, 8), 8), :]
            xr, xi = x[:, 0:SSM_N], x[:, SSM_N:2 * SSM_N]
            for ki, k in enumerate((1, 2, 4)):
                ar = tab_ref[d, ki, :, 0:SSM_N]
                ai = tab_ref[d, ki, :, SSM_N:2 * SSM_N]
                shift = (8 - k) if reverse else k
                sr = pltpu.roll(xr, shift, 0)
                si = pltpu.roll(xi, shift, 0)
                xr, xi = xr + (ar * sr - ai * si), xi + (ar * si + ai * sr)
            pr = tab_ref[d, 3, :, 0:SSM_N]
            pi = tab_ref[d, 3, :, SSM_N:2 * SSM_N]
            cr = jnp.broadcast_to(carry[:, 0:SSM_N], (8, SSM_N))
            ci = jnp.broadcast_to(carry[:, SSM_N:2 * SSM_N], (8, SSM_N))
            xr, xi = xr + (pr * cr - pi * ci), xi + (pr * ci + pi * cr)
            last = 0 if reverse else 7
            carry = jnp.concatenate([xr[last:last + 1], xi[last:last + 1]], axis=1)
            halves[half] = jnp.concatenate([xr, xi], axis=1)
        hb_ref[pl.ds(r, 16), :] = jnp.concatenate(halves, axis=0).astype(BF16)
        return carry

    return lax.fori_loop(0, n16, body, carry)


def _ssm_kernel(uf_ref, ub_ref, bmat_ref, cmat_ref, tab_ref, h0_ref, yf_ref, yb_ref, fin_ref,
                s_ref, hb_ref, carry_ref):
    nseq = uf_ref.shape[0]
    rows = SSM_GROUP * SSM_CHUNK
    step = 2 * SSM_CHUNK

    @pl.when(pl.program_id(1) == 0)
    def _():
        carry_ref[...] = h0_ref[...]

    for d, u_ref, y_ref in ((0, uf_ref, yf_ref), (1, ub_ref, yb_ref)):
        for g0 in range(0, nseq, SSM_GROUP):
            u_bf = u_ref[g0:g0 + SSM_GROUP].reshape(rows, SSM_W).astype(BF16)
            for r in range(0, rows, step):
                s_ref[r:r + step, :] = _dot(u_bf[r:r + step], bmat_ref[d])
            for b in range(SSM_GROUP):
                carry = carry_ref[d, g0 + b:g0 + b + 1, :]
                carry = _tile_scan(s_ref, hb_ref, tab_ref, d, b * SSM_CHUNK, carry, d == 1)
                carry_ref[d, g0 + b:g0 + b + 1, :] = carry
            for r in range(0, rows, step):
                y = _dot(hb_ref[r:r + step, :], cmat_ref[d])
                for b in range(step // SSM_CHUNK):
                    y_ref[g0 + r // SSM_CHUNK + b] = y[b * SSM_CHUNK:(b + 1) * SSM_CHUNK]
    fin_ref[...] = carry_ref[...]


def _ssm(u_fwd_spec, u_bwd_spec, y_fwd_spec, y_bwd_spec, y_shape, grid, nseq, nseq_total, name):
    state = pl.BlockSpec((2, nseq, 2 * SSM_N), lambda g, c: (0, g, 0))
    rows = SSM_GROUP * SSM_CHUNK
    return pl.pallas_call(
        _ssm_kernel,
        grid=grid,
        in_specs=[u_fwd_spec, u_bwd_spec, _full((2, SSM_W, 2 * SSM_N)), _full((2, 2 * SSM_N, SSM_W)),
                  _full((2, 4, 8, 2 * SSM_N)), state],
        out_specs=[y_fwd_spec, y_bwd_spec, state],
        out_shape=[jax.ShapeDtypeStruct(y_shape, F32), jax.ShapeDtypeStruct(y_shape, F32),
                   jax.ShapeDtypeStruct((2, nseq_total, 2 * SSM_N), F32)],
        scratch_shapes=[pltpu.VMEM((rows, 2 * SSM_N), F32), pltpu.VMEM((rows, 2 * SSM_N), BF16),
                        pltpu.VMEM((2, nseq, 2 * SSM_N), F32)],
        compiler_params=_params(("parallel", "arbitrary")),
        name=name,
    )


def _ssm_ctx(u, bmat, cmat, tab, h0):
    nseq = 8
    blk = pl.BlockSpec((nseq, CTX_L, SSM_W), lambda g, c: (N_LAT_TILES // nseq + g, 0, 0))
    out = pl.BlockSpec((nseq, CTX_L, SSM_W), lambda g, c: (g, 0, 0))
    call = _ssm(blk, blk, out, out, (N_CTX_B, CTX_L, SSM_W), (N_CTX_B // nseq, 1), nseq, N_CTX_B, "ssm_ctx")
    u3 = u.reshape(T_ALL // TM, TM, SSM_W)
    return call(u3, u3, bmat, cmat, tab, h0)


def _ssm_lat(u, bmat, cmat, tab, h0):
    nc = LAT_L // SSM_CHUNK
    fwd = pl.BlockSpec((N_LAT_B, None, SSM_CHUNK, SSM_W), lambda g, c: (0, c, 0, 0))
    bwd = pl.BlockSpec((N_LAT_B, None, SSM_CHUNK, SSM_W), lambda g, c: (0, nc - 1 - c, 0, 0))
    call = _ssm(fwd, bwd, fwd, bwd, (N_LAT_B, nc, SSM_CHUNK, SSM_W), (1, nc), N_LAT_B, N_LAT_B, "ssm_lat")
    u4 = u.reshape(T_ALL // LAT_L, nc, SSM_CHUNK, SSM_W)
    return call(u4, u4, bmat, cmat, tab, h0)


def _gelu_tanh(x):
    return 0.5 * x * (1.0 + jnp.tanh(math.sqrt(2.0 / math.pi) * (x + 0.044715 * (x * x * x))))


def _outproj_kernel(x_ref, u_ref, oal_ref, oac_ref, ocl_ref, occ_ref, yfl_ref, yfc_ref, ybl_ref, ybc_ref,
                    mod_ref, d_ref, wglu_ref, woa_ref, wob_ref, woc_ref, gffn_ref, wrhi_ref, wrlo_ref, rb_ref,
                    x1_ref, h2_ref, comb_ref):
    is_lat = pl.program_id(0) < N_LAT_TILES
    oa = jnp.where(is_lat, oal_ref[...], oac_ref[...])
    oc = jnp.where(is_lat, ocl_ref[...], occ_ref[...])
    y = d_ref[...] * u_ref[...] + jnp.where(is_lat, yfl_ref[...] + ybl_ref[...], yfc_ref[...] + ybc_ref[...])
    g = _gelu_tanh(y)
    ob = g * jax.nn.sigmoid(_dot(g.astype(BF16), wglu_ref[...]))
    mix = _dot(oa, woa_ref[...]) + _dot(ob.astype(BF16), wob_ref[...]) + _dot(oc, woc_ref[...])
    x1 = x_ref[...] + mod_ref[2:3, :] * mix
    x1_ref[...] = x1
    ms = jnp.mean(x1 * x1, axis=-1, keepdims=True)
    h2 = x1 * lax.rsqrt(ms + EPS) * gffn_ref[...]
    h2 = h2 * (1.0 + mod_ref[4:5, :]) + mod_ref[3:4, :]
    h2_hi, h2_lo = _split_bf16(h2)
    h2_ref[...] = h2_hi

    lt = _dot_nt(wrhi_ref[...], h2_hi) + _dot_nt(wrhi_ref[...], h2_lo) + _dot_nt(wrlo_ref[...], h2_hi)
    sc = [jax.nn.sigmoid(lt[e:e + 1, :]) for e in range(N_EXP)]
    bi = [sc[e] + rb_ref[e:e + 1, 0:1] for e in range(N_EXP)]
    gs = []
    for gi in range(N_GRP):
        v = bi[gi * GRP_SZ:(gi + 1) * GRP_SZ]
        best2 = None
        for i in range(GRP_SZ):
            for j in range(i + 1, GRP_SZ):
                pair = v[i] + v[j]
                best2 = pair if best2 is None else jnp.maximum(best2, pair)
        gs.append(best2)
    best_g = jnp.zeros((1, TM), jnp.int32)
    best_v = gs[0]
    for gi in range(1, N_GRP):
        upd = gs[gi] > best_v
        best_g = jnp.where(upd, gi, best_g)
        best_v = jnp.where(upd, gs[gi], best_v)
    wts = []
    for e in range(N_EXP):
        gi = e // GRP_SZ
        beaten = jnp.zeros((1, TM), F32)
        for j in range(gi * GRP_SZ, (gi + 1) * GRP_SZ):
            if j == e:
                continue
            if j < e:
                beaten = beaten + jnp.where(bi[j] >= bi[e], 1.0, 0.0)
            else:
                beaten = beaten + jnp.where(bi[j] > bi[e], 1.0, 0.0)
        keep = jnp.where(best_g == gi, jnp.where(beaten < 1.5, 1.0, 0.0), 0.0)
        wts.append(keep * sc[e])
    tot = wts[0]
    for e in range(1, N_EXP):
        tot = tot + wts[e]
    inv = 1.0 / tot
    comb_t = jnp.concatenate([w_ * inv for w_ in wts] + [jnp.zeros((LANES - N_EXP, TM), F32)], axis=0)
    comb_ref[...] = comb_t.T


def _outproj(x, u, oa_lat, oa_ctx, oc_lat, oc_ctx, yf_lat, yf_ctx, yb_lat, yb_ctx, mod_l, w, wr_hi, wr_lo, rb):
    nt = T_ALL // TM
    tile = lambda width: pl.BlockSpec((TM, width), lambda i: (i, 0))
    lat = lambda width: pl.BlockSpec((TM, width), lambda i: (jnp.minimum(i, N_LAT_TILES - 1), 0))
    ctx = lambda width: pl.BlockSpec((TM, width), lambda i: (jnp.maximum(i - N_LAT_TILES, 0), 0))
    in_specs = [
        tile(D), tile(256), lat(512), ctx(512), lat(256), ctx(256), lat(256), ctx(256), lat(256), ctx(256),
        pl.BlockSpec((None, 6, D), lambda i: (_mod_row(i), 0, 0)),
        _full((1, SSM_W)), _full((SSM_W, SSM_W)),
        _full((512, D)), _full((256, D)), _full((256, D)), _full((1, D)),
        _full((LANES, D)), _full((LANES, D)), _full((N_EXP, LANES)),
    ]
    return pl.pallas_call(
        _outproj_kernel,
        grid=(nt,),
        in_specs=in_specs,
        out_specs=[tile(D), tile(D), tile(LANES)],
        out_shape=[jax.ShapeDtypeStruct((T_ALL, D), F32), jax.ShapeDtypeStruct((T_ALL, D), BF16),
                   jax.ShapeDtypeStruct((T_ALL, LANES), F32)],
        compiler_params=_params(("parallel",)),
        name="outproj_router",
    )(x, u, oa_lat, oa_ctx, oc_lat, oc_ctx, yf_lat, yf_ctx, yb_lat, yb_ctx, mod_l, w["ssm_d"], w["wglu"],
      w["woa"], w["wob"], w["woc"], w["gffn"], wr_hi, wr_lo, rb)


def _moe_kernel(h_ref, comb_ref, x1_ref, mod_ref, wg_ref, wu_ref, wd_ref, o_ref, acc_ref):
    e = pl.program_id(1)

    @pl.when(e == 0)
    def _():
        acc_ref[...] = jnp.zeros_like(acc_ref)

    h = h_ref[...]
    a = _dot(h, wg_ref[...])
    b = _dot(h, wu_ref[...])
    act = (a * jax.nn.sigmoid(a)) * b
    y = _dot(act.astype(BF16), wd_ref[...])
    lane = lax.broadcasted_iota(jnp.int32, (TM_MOE, LANES), 1)
    wcol = jnp.sum(jnp.where(lane == e, comb_ref[...], 0.0), axis=-1, keepdims=True)
    acc_ref[...] += wcol * y

    @pl.when(e == N_EXP - 1)
    def _():
        o_ref[...] = x1_ref[...] + mod_ref[5:6, :] * acc_ref[...]


def _moe(h2, comb, x1, mod_l, wg, wu, wd):
    nt = T_ALL // TM_MOE
    per = TM_MOE // TM
    tile = lambda width: pl.BlockSpec((TM_MOE, width), lambda i, e: (i, 0))
    return pl.pallas_call(
        _moe_kernel,
        grid=(nt, N_EXP),
        in_specs=[
            tile(D), tile(LANES), tile(D),
            pl.BlockSpec((None, 6, D), lambda i, e: (_mod_row(i * per), 0, 0)),
            pl.BlockSpec((None, D, D_EXP), lambda i, e: (e, 0, 0)),
            pl.BlockSpec((None, D, D_EXP), lambda i, e: (e, 0, 0)),
            pl.BlockSpec((None, D_EXP, D), lambda i, e: (e, 0, 0)),
        ],
        out_specs=tile(D),
        out_shape=jax.ShapeDtypeStruct((T_ALL, D), F32),
        scratch_shapes=[pltpu.VMEM((TM_MOE, D), F32)],
        compiler_params=_params(("parallel", "arbitrary")),
        name="moe",
    )(h2, comb, x1, mod_l, wg, wu, wd)


def _pad_heads(w, heads, dim):
    k = w.shape[0]
    return jnp.pad(w.reshape(k, heads, dim), ((0, 0), (0, 0), (0, LANES - dim))).reshape(k, heads * LANES)


def _block_diag_b(bb_t):
    eye = jnp.eye(SSM_G, dtype=bb_t.dtype)
    return jnp.einsum("ghp,gk->ghkp", bb_t, eye).reshape(SSM_W, SSM_N)


def _block_diag_c(c):
    eye = jnp.eye(SSM_G, dtype=c.dtype)
    return jnp.einsum("ghp,gk->gpkh", c, eye).reshape(SSM_N, SSM_W)


def _scan_tables(pw_re, pw_im):
    def flat(p):
        return jnp.transpose(p, (0, 2, 1, 3)).reshape(2, 8, SSM_N)

    pw = jnp.concatenate([flat(pw_re), flat(pw_im)], axis=-1)
    pos = jnp.arange(8)[:, None]
    tabs = []
    for d in range(2):
        rows = []
        for k in (1, 2, 4):
            keep = (pos >= k) if d == 0 else (pos <= 7 - k)
            rows.append(jnp.where(keep, pw[d, k - 1][None, :], 0.0))
        rows.append(pw[d] if d == 0 else pw[d, ::-1])
        tabs.append(jnp.stack(rows))
    return jnp.stack(tabs)


def kernel(x_prompt, x_sample, cache_attn_k, cache_attn_v, cache_mla_ckv, cache_mla_kpe, state_ssm_re,
           state_ssm_im, c, c_ctx, norm_mix, norm_ffn, w_ada, b_ada, w_in, a_q_norm, a_k_norm, a_sink,
           ssm_a_re, ssm_a_im, ssm_log_dt, ssm_b_re, ssm_b_im, ssm_c_re, ssm_c_im, ssm_d, w_glu,
           mla_q_norm, mla_kv_norm, w_q_b, w_kv_b, mla_qh_norm, mla_kh_norm, w_out, w_router,
           router_bias, w_gate, w_up, w_down):
    x = jnp.concatenate([x_sample.reshape(T_LAT, D), x_prompt.reshape(T_CTX, D)], axis=0)

    cond = jnp.concatenate([c_ctx[None, :], c, jnp.zeros((16 - 1 - N_LAT_B, D), F32)], axis=0)
    mod = _adaln(cond, w_ada, b_ada)

    pw_re, pw_im, bbt_re, bbt_im = _discretize(ssm_a_re, ssm_a_im, ssm_log_dt, ssm_b_re, ssm_b_im)
    pw_re = pw_re.reshape(DEPTH, 2, SSM_G, 8, SSM_P)
    pw_im = pw_im.reshape(DEPTH, 2, SSM_G, 8, SSM_P)
    bbt_re = bbt_re.reshape(DEPTH, 2, SSM_G, SSM_H, SSM_P)
    bbt_im = bbt_im.reshape(DEPTH, 2, SSM_G, SSM_H, SSM_P)
    h0_ctx = jnp.zeros((2, N_CTX_B, 2 * SSM_N), F32)

    tab_a, tab_c = _rope_tables()

    wr_t = jnp.pad(w_router.T, ((0, LANES - N_EXP), (0, 0)))
    wr_hi = wr_t.astype(BF16)
    wr_lo = (wr_t - wr_hi.astype(F32)).astype(BF16)
    rb = jnp.broadcast_to(router_bias[:, None], (N_EXP, LANES))

    w_kv4 = w_kv_b.reshape(DEPTH, C_KVLORA, C_HEADS, C_NOPE + C_V)
    wkn_all = jnp.pad(w_kv4[..., :C_NOPE], ((0, 0), (0, 0), (0, 0), (0, LANES - C_NOPE))
                      ).reshape(DEPTH, C_KVLORA, C_HEADS * LANES).astype(BF16)
    wv_all = w_kv4[..., C_NOPE:].reshape(DEPTH, C_KVLORA, C_HEADS * C_V).astype(BF16)
    gkh_all = jnp.pad(mla_kh_norm, ((0, 0), (0, LANES - C_QK))).reshape(DEPTH, 1, LANES)

    kpe_slab = jnp.pad(cache_mla_kpe, ((0, 0), (0, 0), (0, 0), (C_NOPE, LANES - C_QK)))
    kx_all, vx_all = _cache_kv(cache_mla_ckv, kpe_slab, wkn_all, wv_all, gkh_all)

    cache_k = cache_attn_k.reshape(N_LAT_B, DEPTH, CTX_L, A_KV * A_DH)
    cache_v = cache_attn_v.reshape(N_LAT_B, DEPTH, CTX_L, A_KV * A_DH)

    ks, vs, ckvs, kpes, fins = [], [], [], [], []
    for l in range(DEPTH):
        wl = w_in[l]
        w = {
            "gmix": norm_mix[l][None, :],
            "wq": wl[:, 0:512].astype(BF16),
            "wkv": wl[:, 512:768].astype(BF16),
            "wu": wl[:, 768:1024].astype(BF16),
            "wcq": wl[:, 1024:1280].astype(BF16),
            "wt": jnp.pad(wl[:, 1280:1440], ((0, 0), (0, 256 - 160))).astype(BF16),
            "gq": jnp.tile(a_q_norm[l], 2)[None, :],
            "gk": jnp.tile(a_k_norm[l], 2)[None, :],
            "gcq": mla_q_norm[l][None, :],
            "gckv": mla_kv_norm[l][None, :],
            "wqb": _pad_heads(w_q_b[l], C_HEADS, C_QK).astype(BF16),
            "wkn": wkn_all[l],
            "wv": wv_all[l],
            "gqh": jnp.pad(mla_qh_norm[l], (0, LANES - C_QK))[None, :],
            "gkh": gkh_all[l],
            "ssm_d": ssm_d[l].reshape(1, SSM_W),
            "wglu": w_glu[l].astype(BF16),
            "woa": w_out[l, 0:512].astype(BF16),
            "wob": w_out[l, 512:768].astype(BF16),
            "woc": w_out[l, 768:1024].astype(BF16),
            "gffn": norm_ffn[l][None, :],
        }
        mod_l = mod[l]
        qa, ka, va, u, qc, kc, vc, ckv_n, kpe = _proj(x, mod_l, w, tab_a, tab_c)

        sink = jnp.broadcast_to(a_sink[l][:, None], (A_HEADS, LANES))
        oa_ctx = _attn_a_ctx(qa, ka, va, sink)
        oa_lat = _attn_a_lat(qa, ka, va, cache_k[:, l], cache_v[:, l], sink)
        oc_ctx = _mla_ctx(qc, kc, vc)
        oc_lat = _mla_lat(qc, kc, vc, kx_all[l], vx_all[l])

        bmat = jnp.stack([jnp.concatenate([_block_diag_b(bbt_re[l, d]), _block_diag_b(bbt_im[l, d])], axis=1)
                          for d in range(2)]).astype(BF16)
        cmat = jnp.stack([jnp.concatenate([_block_diag_c(ssm_c_re[l, d]), -_block_diag_c(ssm_c_im[l, d])], axis=0)
                          for d in range(2)]).astype(BF16)
        tab = _scan_tables(pw_re[l], pw_im[l])
        yf_ctx, yb_ctx, fin = _ssm_ctx(u, bmat, cmat, tab, h0_ctx)
        h0 = jnp.swapaxes(jnp.concatenate([state_ssm_re[:, l].reshape(N_LAT_B, 2, SSM_N),
                                            state_ssm_im[:, l].reshape(N_LAT_B, 2, SSM_N)], axis=-1), 0, 1)
        yf_lat, yb_lat, _ = _ssm_lat(u, bmat, cmat, tab, h0)

        x1, h2, comb = _outproj(x, u, oa_lat, oa_ctx, oc_lat, oc_ctx,
                                yf_lat.reshape(T_LAT, SSM_W), yf_ctx.reshape(T_CTX, SSM_W),
                                yb_lat.reshape(T_LAT, SSM_W), yb_ctx.reshape(T_CTX, SSM_W),
                                mod_l, w, wr_hi, wr_lo, rb)
        x = _moe(h2, comb, x1, mod_l, w_gate[l].astype(BF16), w_up[l].astype(BF16), w_down[l].astype(BF16))

        ks.append(ka[T_LAT:].reshape(N_CTX_B, CTX_L, A_KV, A_DH))
        vs.append(va[T_LAT:].reshape(N_CTX_B, CTX_L, A_KV, A_DH))
        ckvs.append(ckv_n[T_LAT:].reshape(N_CTX_B, CTX_L, C_KVLORA))
        kpes.append(kpe[T_LAT:].reshape(N_CTX_B, CTX_L, C_ROPE))
        fins.append(fin)

    y_prompt = x[T_LAT:].reshape(N_CTX_B, CTX_L, D)
    y_sample = x[:T_LAT].reshape(N_LAT_B, LAT_L, D)
    fin = jnp.transpose(jnp.stack(fins, axis=0), (2, 0, 1, 3))
    new_re = fin[..., 0:SSM_N].reshape(N_CTX_B, DEPTH, 2, SSM_G, SSM_P)
    new_im = fin[..., SSM_N:].reshape(N_CTX_B, DEPTH, 2, SSM_G, SSM_P)
    return (y_prompt, y_sample, jnp.stack(ks, axis=1), jnp.stack(vs, axis=1), jnp.stack(ckvs, axis=1),
            jnp.stack(kpes, axis=1), new_re, new_im)
```

```python
import functools
import math

import jax
import jax.numpy as jnp
from jax import lax
from jax.experimental import pallas as pl
from jax.experimental.pallas import tpu as pltpu

F32 = jnp.float32
BF16 = jnp.bfloat16

D = 1024
N_CTX_B, CTX_L = 32, 256
N_LAT_B, LAT_L = 4, 4096
DEPTH = 4
GRID_W = 64
EPS = 1e-6
ROPE_BASE = 10000.0
A_HEADS, A_KV, A_DH = 8, 2, 64
SSM_G, SSM_H, SSM_P = 16, 16, 64
SSM_W = SSM_G * SSM_H
SSM_N = SSM_G * SSM_P
SSM_SLABS = SSM_N // 128
C_HEADS, C_NOPE, C_ROPE, C_V = 4, 64, 32, 64
C_QK = C_NOPE + C_ROPE
C_QLORA, C_KVLORA = 256, 128
N_EXP, N_GRP, GRP_SZ, D_EXP = 16, 4, 4, 512
WINDOW = 128

T_CTX = N_CTX_B * CTX_L
T_LAT = N_LAT_B * LAT_L
T_ALL = T_CTX + T_LAT

LANES = 128
HALF = LANES // 2
VMEM_LIMIT = 48 * 1024 * 1024

TM = 512
N_LAT_TILES = T_LAT // TM
LAT_TILES_PER_B = LAT_L // TM
LAT_SEQ_BLOCKS = T_LAT // CTX_L
TQ_A = 256
TQ_C = 256
TM_MOE = 1024
MOE_SHIFT = 7
MOE_CHUNK = 1 << MOE_SHIFT
MOE_SLOTS = TM_MOE + N_GRP * MOE_CHUNK
MOE_VMEM_LIMIT = 56 * 1024 * 1024
SSM_CHUNK = 256
SSM_GROUP = 4
NEG = -1e30
LOG2E = math.log2(math.e)


def _dot(a, b):
    return jnp.dot(a, b, preferred_element_type=F32)


def _dot_nt(a, b):
    return lax.dot_general(a, b, (((1,), (1,)), ((), ())), preferred_element_type=F32)


def _split_bf16(x):
    hi = x.astype(BF16)
    lo = (x - hi.astype(F32)).astype(BF16)
    return hi, lo


def _params(sem):
    return pltpu.CompilerParams(dimension_semantics=sem, vmem_limit_bytes=VMEM_LIMIT)


def _full(shape):
    n = len(shape)
    return pl.BlockSpec(shape, lambda *_: (0,) * n)


def _mod_row(i):
    return jnp.where(i < N_LAT_TILES, 1 + i // LAT_TILES_PER_B, 0)


def _pos_block(i):
    return jnp.where(i < N_LAT_TILES, i % LAT_TILES_PER_B, LAT_TILES_PER_B)


def _adaln_kernel(c_ref, w_ref, b_ref, o_ref):
    c = c_ref[...]
    s = c * jax.nn.sigmoid(c)
    s_hi, s_lo = _split_bf16(s)
    w_hi, w_lo = _split_bf16(w_ref[...])
    o_ref[...] = _dot(s_hi, w_hi) + _dot(s_hi, w_lo) + _dot(s_lo, w_hi) + b_ref[...]


def _adaln(cond, w_ada, b_ada):
    nt = 6
    out = pl.pallas_call(
        _adaln_kernel,
        grid=(DEPTH, nt),
        in_specs=[
            pl.BlockSpec((16, D), lambda l, n: (0, 0)),
            pl.BlockSpec((None, D, D), lambda l, n: (l, 0, n)),
            pl.BlockSpec((None, 1, D), lambda l, n: (l, 0, n)),
        ],
        out_specs=pl.BlockSpec((None, 16, D), lambda l, n: (l, 0, n)),
        out_shape=jax.ShapeDtypeStruct((DEPTH, 16, nt * D), F32),
        compiler_params=_params(("parallel", "parallel")),
        name="adaln",
    )(cond, w_ada, b_ada.reshape(DEPTH, 1, nt * D))
    return out.reshape(DEPTH, 16, nt, D)


def _disc_kernel(are_ref, aim_ref, ldt_ref, bre_ref, bim_ref, pwre_ref, pwim_ref, bbre_ref, bbim_ref):
    a_re = are_ref[...]
    a_im = aim_ref[...]
    dt = jnp.exp(ldt_ref[...])
    mag = jnp.exp(a_re * dt)
    ab_re = mag * jnp.cos(a_im * dt)
    ab_im = mag * jnp.sin(a_im * dt)
    den = a_re * a_re + a_im * a_im
    n_re = ab_re - 1.0
    k_re = (n_re * a_re + ab_im * a_im) / den
    k_im = (ab_im * a_re - n_re * a_im) / den
    b_re = bre_ref[...]
    b_im = bim_ref[...]
    bbre_ref[...] = k_re * b_re - k_im * b_im
    bbim_ref[...] = k_re * b_im + k_im * b_re
    p_re, p_im = ab_re, ab_im
    for k in range(8):
        pwre_ref[:, k:k + 1, :] = p_re
        pwim_ref[:, k:k + 1, :] = p_im
        p_re, p_im = p_re * ab_re - p_im * ab_im, p_re * ab_im + p_im * ab_re


def _discretize(ssm_a_re, ssm_a_im, ssm_log_dt, ssm_b_re, ssm_b_im):
    n = DEPTH * 2 * SSM_G
    a_re = ssm_a_re.reshape(n, 1, SSM_P)
    a_im = ssm_a_im.reshape(n, 1, SSM_P)
    ldt = jnp.broadcast_to(ssm_log_dt.reshape(n, 1, 1), (n, 1, SSM_P))
    b_re = jnp.swapaxes(ssm_b_re.reshape(n, SSM_P, SSM_H), 1, 2)
    b_im = jnp.swapaxes(ssm_b_im.reshape(n, SSM_P, SSM_H), 1, 2)
    sa = jax.ShapeDtypeStruct((n, 8, SSM_P), F32)
    sb = jax.ShapeDtypeStruct((n, SSM_H, SSM_P), F32)
    return pl.pallas_call(
        _disc_kernel,
        out_shape=(sa, sa, sb, sb),
        compiler_params=pltpu.CompilerParams(vmem_limit_bytes=VMEM_LIMIT),
        name="s5_discretize",
    )(a_re, a_im, ldt, b_re, b_im)


def _rope_tables():
    t = jnp.arange(LAT_L)
    row = (t // GRID_W).astype(F32)[:, None]
    col = (t % GRID_W).astype(F32)[:, None]
    lane = jnp.arange(LANES)

    def build(d, width, valid):
        half = width // 2
        q = half // 2
        on_col = d >= half
        dd = jnp.where(on_col, d - half, d)
        first = dd < q
        j = jnp.where(first, dd, dd - q).astype(F32)
        inv = ROPE_BASE ** (-j / q)
        ang = jnp.where(on_col[None, :], col, row) * inv[None, :]
        cos = jnp.where(valid[None, :], jnp.cos(ang), 1.0)
        sin = jnp.where(valid[None, :], jnp.sin(ang), 0.0)
        s_up = jnp.where(first[None, :], -sin, 0.0)
        s_dn = jnp.where(first[None, :], 0.0, sin)
        tab = jnp.concatenate([cos, s_up, s_dn], axis=1)
        ident = jnp.concatenate([jnp.ones((TM, LANES), F32), jnp.zeros((TM, 2 * LANES), F32)], axis=1)
        return jnp.concatenate([tab, ident], axis=0)

    tab_a = build(lane % A_DH, A_DH, jnp.ones((LANES,), bool))
    dc = jnp.clip(lane - C_NOPE, 0, C_ROPE - 1)
    tab_c = build(dc, C_ROPE, (lane >= C_NOPE) & (lane < C_QK))
    return tab_a, tab_c


def _rope(x, tab_ref, shift):
    cos = tab_ref[:, 0:LANES]
    s_up = tab_ref[:, LANES:2 * LANES]
    s_dn = tab_ref[:, 2 * LANES:3 * LANES]
    return x * cos + pltpu.roll(x, LANES - shift, 1) * s_up + pltpu.roll(x, shift, 1) * s_dn


def _half_norm(xs, gain):
    lo = lax.broadcasted_iota(jnp.int32, xs.shape, 1) < A_DH
    sq = xs * xs
    s_lo = jnp.sum(jnp.where(lo, sq, 0.0), axis=-1, keepdims=True)
    s_hi = jnp.sum(jnp.where(lo, 0.0, sq), axis=-1, keepdims=True)
    r = jnp.where(lo, lax.rsqrt(s_lo * (1.0 / A_DH) + EPS), lax.rsqrt(s_hi * (1.0 / A_DH) + EPS))
    return xs * r * gain


def _with_ones(v, heads):
    lo = lax.broadcasted_iota(jnp.int32, (v.shape[0], LANES), 1) < HALF
    slabs = []
    for h in range(heads):
        pair = v[:, (h // 2) * LANES:(h // 2 + 1) * LANES]
        own = pair if h % 2 == 0 else pltpu.roll(pair, HALF, 1)
        slabs.append(jnp.where(lo, own, 1.0).astype(BF16))
    return slabs


def _normalise(o):
    return o / pltpu.roll(o, HALF, 1)


def _store_head_pairs(rs, o_ref):
    lo = lax.broadcasted_iota(jnp.int32, rs[0].shape, 1) < HALF
    for j in range(len(rs) // 2):
        slab = jnp.where(lo, rs[2 * j], pltpu.roll(rs[2 * j + 1], HALF, 1))
        o_ref[:, j * LANES:(j + 1) * LANES] = slab.astype(o_ref.dtype)


def _mla_keys_values(ckv_n, kpe_slab, wkn_ref, wv_ref, gkh, tab_ref):
    cb = ckv_n.astype(BF16)
    kn = _dot(cb, wkn_ref[...])
    v = _dot(cb, wv_ref[...])
    ks = []
    for h in range(C_HEADS):
        slab = kn[:, h * LANES:(h + 1) * LANES] + kpe_slab
        ss = jnp.sum(slab * slab, axis=-1, keepdims=True)
        slab = slab * lax.rsqrt(ss * (1.0 / C_QK) + EPS) * gkh
        if tab_ref is not None:
            slab = _rope(slab, tab_ref, C_ROPE // 4)
        ks.append(slab.astype(BF16))
    return ks, v


def _proj_kernel(x_ref, mod_ref, gmix_ref, wq_ref, wkv_ref, wu_ref, wcq_ref, wt_ref,
                 gq_ref, gk_ref, gcq_ref, gckv_ref, wqb_ref, wkn_ref, wv_ref, gqh_ref, gkh_ref,
                 ta_ref, tc_ref,
                 qa_ref, ka_ref, va_ref, vaug_ref, u_ref, qc_ref, kc_ref, vc_ref, ckv_ref, kpe_ref):
    x = x_ref[...]
    sh = mod_ref[0:1, :]
    sc = mod_ref[1:2, :]
    ms = jnp.mean(x * x, axis=-1, keepdims=True)
    h = x * lax.rsqrt(ms + EPS) * gmix_ref[...]
    h = (h * (1.0 + sc) + sh).astype(BF16)

    lo = lax.broadcasted_iota(jnp.int32, (TM, LANES), 1) < A_DH

    p_q = _dot(h, wq_ref[...])
    for j in range(A_HEADS // 2):
        slab = _half_norm(p_q[:, j * LANES:(j + 1) * LANES], gq_ref[...])
        slab = _rope(slab, ta_ref, A_DH // 4) * (A_DH ** -0.5 * LOG2E)
        swapped = pltpu.roll(slab, A_DH, 1)
        kv_head = (2 * j) // (A_HEADS // A_KV)
        if kv_head == 0:
            q0 = jnp.where(lo, slab, 0.0)
            q1 = jnp.where(lo, swapped, 0.0)
        else:
            q0 = jnp.where(lo, 0.0, swapped)
            q1 = jnp.where(lo, 0.0, slab)
        qa_ref[:, (2 * j) * LANES:(2 * j + 1) * LANES] = q0.astype(BF16)
        qa_ref[:, (2 * j + 1) * LANES:(2 * j + 2) * LANES] = q1.astype(BF16)

    p_kv = _dot(h, wkv_ref[...])
    k = _half_norm(p_kv[:, 0:LANES], gk_ref[...])
    ka_ref[...] = _rope(k, ta_ref, A_DH // 4)
    va_ref[...] = p_kv[:, LANES:2 * LANES]
    for kvh, slab in enumerate(_with_ones(p_kv[:, LANES:2 * LANES], A_KV)):
        vaug_ref[:, kvh * LANES:(kvh + 1) * LANES] = slab

    u_ref[...] = _dot(h, wu_ref[...])

    p_cq = _dot(h, wcq_ref[...])
    ms = jnp.mean(p_cq * p_cq, axis=-1, keepdims=True)
    cq = (p_cq * lax.rsqrt(ms + EPS) * gcq_ref[...]).astype(BF16)
    q = _dot(cq, wqb_ref[...])
    for hd in range(C_HEADS):
        slab = q[:, hd * LANES:(hd + 1) * LANES]
        ss = jnp.sum(slab * slab, axis=-1, keepdims=True)
        slab = slab * lax.rsqrt(ss * (1.0 / C_QK) + EPS) * gqh_ref[...]
        slab = _rope(slab, tc_ref, C_ROPE // 4) * (C_QK ** -0.5 * LOG2E)
        qc_ref[:, hd * LANES:(hd + 1) * LANES] = slab.astype(BF16)

    p_t = _dot(h, wt_ref[...])
    ckv = p_t[:, 0:LANES]
    ms = jnp.mean(ckv * ckv, axis=-1, keepdims=True)
    ckv_n = ckv * lax.rsqrt(ms + EPS) * gckv_ref[...]
    ckv_ref[...] = ckv_n
    kpe_wide = p_t[:, LANES:2 * LANES]
    kpe_ref[...] = kpe_wide[:, 0:C_ROPE]
    kpe_slab = pltpu.roll(kpe_wide, C_NOPE, 1)
    ks, v = _mla_keys_values(ckv_n, kpe_slab, wkn_ref, wv_ref, gkh_ref[...], tc_ref)
    for hd, slab in enumerate(_with_ones(v, C_HEADS)):
        kc_ref[:, hd * LANES:(hd + 1) * LANES] = ks[hd]
        vc_ref[:, hd * LANES:(hd + 1) * LANES] = slab


def _proj(x, mod_l, w, tab_a, tab_c):
    nt = T_ALL // TM
    tile = lambda width: pl.BlockSpec((TM, width), lambda i: (i, 0))
    tab = pl.BlockSpec((TM, 3 * LANES), lambda i: (_pos_block(i), 0))
    in_specs = [
        tile(D),
        pl.BlockSpec((None, 6, D), lambda i: (_mod_row(i), 0, 0)),
        _full((1, D)),
        _full((D, 512)), _full((D, 256)), _full((D, 256)), _full((D, 256)), _full((D, 256)),
        _full((1, LANES)), _full((1, LANES)), _full((1, 256)), _full((1, LANES)),
        _full((256, 512)), _full((LANES, 512)), _full((LANES, 256)),
        _full((1, LANES)), _full((1, LANES)),
        tab, tab,
    ]
    outs = [(1024, BF16), (128, F32), (128, F32), (256, BF16), (256, F32), (512, BF16), (512, BF16), (512, BF16),
            (128, F32), (C_ROPE, F32)]
    return pl.pallas_call(
        _proj_kernel,
        grid=(nt,),
        in_specs=in_specs,
        out_specs=[tile(wd) for wd, _ in outs],
        out_shape=[jax.ShapeDtypeStruct((T_ALL, wd), dt) for wd, dt in outs],
        compiler_params=_params(("parallel",)),
        name="proj",
    )(x, mod_l, w["gmix"], w["wq"], w["wkv"], w["wu"], w["wcq"], w["wt"],
      w["gq"], w["gk"], w["gcq"], w["gckv"], w["wqb"], w["wkn"], w["wv"], w["gqh"], w["gkh"],
      tab_a, tab_c)


def _cache_kv_kernel(ckv_ref, kpe_ref, wkn_ref, wv_ref, gkh_ref, k_ref, v_ref):
    ks, v = _mla_keys_values(ckv_ref[...], kpe_ref[...], wkn_ref, wv_ref, gkh_ref[...], None)
    for hd, slab in enumerate(_with_ones(v, C_HEADS)):
        k_ref[:, hd * LANES:(hd + 1) * LANES] = ks[hd]
        v_ref[:, hd * LANES:(hd + 1) * LANES] = slab


def _cache_kv(cache_ckv, cache_kpe_slab, wkn, wv, gkh):
    return pl.pallas_call(
        _cache_kv_kernel,
        grid=(DEPTH, N_LAT_B),
        in_specs=[
            pl.BlockSpec((None, None, CTX_L, LANES), lambda l, b: (b, l, 0, 0)),
            pl.BlockSpec((None, None, CTX_L, LANES), lambda l, b: (b, l, 0, 0)),
            pl.BlockSpec((None, LANES, 512), lambda l, b: (l, 0, 0)),
            pl.BlockSpec((None, LANES, 256), lambda l, b: (l, 0, 0)),
            pl.BlockSpec((None, 1, LANES), lambda l, b: (l, 0, 0)),
        ],
        out_specs=[
            pl.BlockSpec((None, None, CTX_L, 512), lambda l, b: (l, b, 0, 0)),
            pl.BlockSpec((None, None, CTX_L, 512), lambda l, b: (l, b, 0, 0)),
        ],
        out_shape=[jax.ShapeDtypeStruct((DEPTH, N_LAT_B, CTX_L, 512), BF16),
                   jax.ShapeDtypeStruct((DEPTH, N_LAT_B, CTX_L, 512), BF16)],
        compiler_params=_params(("parallel", "parallel")),
        name="mla_cache_kv",
    )(cache_ckv, cache_kpe_slab, wkn, wv, gkh)


def _sink_column(sink_ref, kv_head, rows):
    grp = A_HEADS // A_KV
    return jnp.concatenate(
        [jnp.broadcast_to(sink_ref[kv_head * grp + g:kv_head * grp + g + 1, 0:1] * LOG2E, (rows, 1))
         for g in range(grp)],
        axis=0)


def _sink_softmax_head(q_ref, kv_head, rows, kb, vslab, valid, sink_ref):
    grp = A_HEADS // A_KV
    q = jnp.concatenate([q_ref[:, (kv_head * grp + g) * LANES:(kv_head * grp + g + 1) * LANES] for g in range(grp)],
                        axis=0)
    s = _dot_nt(q, kb)
    if valid is not None:
        s = jnp.where(valid, s, NEG)
    sk = _sink_column(sink_ref, kv_head, rows)
    m = jnp.maximum(jnp.max(s, axis=-1, keepdims=True), sk)
    o = _dot(jnp.exp2(s - m).astype(BF16), vslab)
    lo = lax.broadcasted_iota(jnp.int32, o.shape, 1) < HALF
    o = o + jnp.where(lo, 0.0, jnp.exp2(sk - m))
    r = _normalise(o)
    return [r[g * rows:(g + 1) * rows] for g in range(grp)]


def _attn_a_ctx_kernel(q_ref, k_ref, v_ref, sink_ref, o_ref):
    kb = k_ref[...].astype(BF16)
    rs = []
    for kvh in range(A_KV):
        rs.extend(_sink_softmax_head(q_ref, kvh, CTX_L, kb, v_ref[:, kvh * LANES:(kvh + 1) * LANES], None, sink_ref))
    _store_head_pairs(rs, o_ref)


def _attn_a_ctx(qa, ka, va, sink):
    return pl.pallas_call(
        _attn_a_ctx_kernel,
        grid=(N_CTX_B,),
        in_specs=[
            pl.BlockSpec((CTX_L, 1024), lambda b: (LAT_SEQ_BLOCKS + b, 0)),
            pl.BlockSpec((CTX_L, LANES), lambda b: (LAT_SEQ_BLOCKS + b, 0)),
            pl.BlockSpec((CTX_L, A_KV * LANES), lambda b: (LAT_SEQ_BLOCKS + b, 0)),
            _full((A_HEADS, LANES)),
        ],
        out_specs=pl.BlockSpec((CTX_L, 512), lambda b: (b, 0)),
        out_shape=jax.ShapeDtypeStruct((T_CTX, 512), BF16),
        compiler_params=_params(("parallel",)),
        name="attn_a_ctx",
    )(qa, ka, va, sink)


def _attn_a_lat_kernel(q_ref, kp_ref, kc_ref, kn_ref, vp_ref, vc_ref, vn_ref, kx_ref, vx_ref, sink_ref, o_ref):
    grp = A_HEADS // A_KV
    n = pl.program_id(1)
    nb = LAT_L // TQ_A
    kb = jnp.concatenate([kp_ref[...], kc_ref[...], kn_ref[...], kx_ref[...]], axis=0).astype(BF16)
    vx = _with_ones(vx_ref[...], A_KV)
    rows = grp * TQ_A
    nkeys = TQ_A + 2 * WINDOW + CTX_L
    qi = lax.broadcasted_iota(jnp.int32, (rows, nkeys), 0) % TQ_A
    kj = lax.broadcasted_iota(jnp.int32, (rows, nkeys), 1)
    rel = kj - WINDOW - qi
    valid = (jnp.abs(rel) <= WINDOW)
    valid = valid & ((kj >= WINDOW) | (n > 0)) & ((kj < WINDOW + TQ_A) | (n < nb - 1))
    valid = valid | (kj >= TQ_A + 2 * WINDOW)
    rs = []
    for kvh in range(A_KV):
        hs = slice(kvh * LANES, (kvh + 1) * LANES)
        vslab = jnp.concatenate([vp_ref[:, hs], vc_ref[:, hs], vn_ref[:, hs], vx[kvh]], axis=0)
        rs.extend(_sink_softmax_head(q_ref, kvh, TQ_A, kb, vslab, valid, sink_ref))
    _store_head_pairs(rs, o_ref)


def _attn_a_lat(qa, ka, va, cache_k_l, cache_v_l, sink):
    nb = LAT_L // TQ_A
    base = 0

    per = TQ_A // WINDOW
    nw = LAT_L // WINDOW
    own = lambda width: pl.BlockSpec((TQ_A, width), lambda b, n: (base + b * nb + n, 0))
    prev = lambda width: pl.BlockSpec((WINDOW, width), lambda b, n: (b * nw + jnp.maximum(n * per - 1, 0), 0))
    nxt = lambda width: pl.BlockSpec((WINDOW, width), lambda b, n: (b * nw + jnp.minimum((n + 1) * per, nw - 1), 0))
    cache = pl.BlockSpec((None, CTX_L, LANES), lambda b, n: (b, 0, 0))
    vw = A_KV * LANES
    return pl.pallas_call(
        _attn_a_lat_kernel,
        grid=(N_LAT_B, nb),
        in_specs=[
            own(1024),
            prev(LANES), own(LANES), nxt(LANES), prev(vw), own(vw), nxt(vw),
            cache, cache, _full((A_HEADS, LANES)),
        ],
        out_specs=pl.BlockSpec((TQ_A, 512), lambda b, n: (b * nb + n, 0)),
        out_shape=jax.ShapeDtypeStruct((T_LAT, 512), BF16),
        compiler_params=_params(("parallel", "parallel")),
        name="attn_a_lat",
    )(qa, ka, ka, ka, va, va, va, cache_k_l, cache_v_l, sink)


def _mla_ctx_kernel(q_ref, k_ref, v_ref, o_ref):
    rs = []
    for h in range(C_HEADS):
        hs = slice(h * LANES, (h + 1) * LANES)
        s = _dot_nt(q_ref[:, hs], k_ref[:, hs])
        m = jnp.max(s, axis=-1, keepdims=True)
        rs.append(_normalise(_dot(jnp.exp2(s - m).astype(BF16), v_ref[:, hs])))
    _store_head_pairs(rs, o_ref)


def _mla_ctx(qc, kc, vc):
    return pl.pallas_call(
        _mla_ctx_kernel,
        grid=(N_CTX_B,),
        in_specs=[
            pl.BlockSpec((CTX_L, 512), lambda b: (LAT_SEQ_BLOCKS + b, 0)),
            pl.BlockSpec((CTX_L, 512), lambda b: (LAT_SEQ_BLOCKS + b, 0)),
            pl.BlockSpec((CTX_L, 512), lambda b: (LAT_SEQ_BLOCKS + b, 0)),
        ],
        out_specs=pl.BlockSpec((CTX_L, 256), lambda b: (b, 0)),
        out_shape=jax.ShapeDtypeStruct((T_CTX, 256), BF16),
        compiler_params=_params(("parallel",)),
        name="mla_ctx",
    )(qc, kc, vc)


def _mla_lat_kernel(q_ref, k_ref, v_ref, kx_ref, vx_ref, o_ref):
    rs = []
    for h in range(C_HEADS):
        hs = slice(h * LANES, (h + 1) * LANES)
        q = q_ref[:, hs]
        s_lat = _dot_nt(q, k_ref[:, hs])
        s_ctx = _dot_nt(q, kx_ref[:, hs])
        m = jnp.max(jnp.concatenate([s_lat, s_ctx], axis=1), axis=-1, keepdims=True)
        o = (_dot(jnp.exp2(s_lat - m).astype(BF16), v_ref[:, hs])
             + _dot(jnp.exp2(s_ctx - m).astype(BF16), vx_ref[:, hs]))
        rs.append(_normalise(o))
    _store_head_pairs(rs, o_ref)


def _mla_lat(qc, kc, vc, kx_l, vx_l):
    nq = LAT_L // TQ_C
    qbase = 0
    kbase = 0
    return pl.pallas_call(
        _mla_lat_kernel,
        grid=(N_LAT_B, nq),
        in_specs=[
            pl.BlockSpec((TQ_C, 512), lambda b, n: (qbase + b * nq + n, 0)),
            pl.BlockSpec((LAT_L, 512), lambda b, n: (kbase + b, 0)),
            pl.BlockSpec((LAT_L, 512), lambda b, n: (kbase + b, 0)),
            pl.BlockSpec((None, CTX_L, 512), lambda b, n: (b, 0, 0)),
            pl.BlockSpec((None, CTX_L, 512), lambda b, n: (b, 0, 0)),
        ],
        out_specs=pl.BlockSpec((TQ_C, 256), lambda b, n: (b * nq + n, 0)),
        out_shape=jax.ShapeDtypeStruct((T_LAT, 256), BF16),
        compiler_params=_params(("parallel", "arbitrary")),
        name="mla_lat",
    )(qc, kc, vc, kx_l, vx_l)


def _tile_scan(s_ref, hb_ref, tab_ref, d, row0, carry, reverse):
    n16 = SSM_CHUNK // 16

    def body(i, carry):
        blk = (n16 - 1 - i) if reverse else i
        r = pl.multiple_of(row0 + blk * 16, 16)
        halves = [None, None]
        for half in ((1, 0) if reverse else (0, 1)):
            x = s_ref[pl.ds(pl.multiple_of(r + half * 8, 8), 8), :]
            xr, xi = x[:, 0:SSM_N], x[:, SSM_N:2 * SSM_N]
            for ki, k in enumerate((1, 2, 4)):
                ar = tab_ref[d, ki, :, 0:SSM_N]
                ai = tab_ref[d, ki, :, SSM_N:2 * SSM_N]
                shift = (8 - k) if reverse else k
                sr = pltpu.roll(xr, shift, 0)
                si = pltpu.roll(xi, shift, 0)
                xr, xi = xr + (ar * sr - ai * si), xi + (ar * si + ai * sr)
            pr = tab_ref[d, 3, :, 0:SSM_N]
            pi = tab_ref[d, 3, :, SSM_N:2 * SSM_N]
            cr = jnp.broadcast_to(carry[:, 0:SSM_N], (8, SSM_N))
            ci = jnp.broadcast_to(carry[:, SSM_N:2 * SSM_N], (8, SSM_N))
            xr, xi = xr + (pr * cr - pi * ci), xi + (pr * ci + pi * cr)
            last = 0 if reverse else 7
            carry = jnp.concatenate([xr[last:last + 1], xi[last:last + 1]], axis=1)
            halves[half] = jnp.concatenate([xr, xi], axis=1)
        hb_ref[pl.ds(r, 16), :] = jnp.concatenate(halves, axis=0).astype(BF16)
        return carry

    return lax.fori_loop(0, n16, body, carry)


def _ssm_kernel(uf_ref, ub_ref, bmat_ref, cmat_ref, tab_ref, h0_ref, yf_ref, yb_ref, fin_ref,
                s_ref, hb_ref, carry_ref):
    nseq = uf_ref.shape[0]
    rows = SSM_GROUP * SSM_CHUNK
    step = 2 * SSM_CHUNK

    @pl.when(pl.program_id(1) == 0)
    def _():
        carry_ref[...] = h0_ref[...]

    for d, u_ref, y_ref in ((0, uf_ref, yf_ref), (1, ub_ref, yb_ref)):
        for g0 in range(0, nseq, SSM_GROUP):
            u_bf = u_ref[g0:g0 + SSM_GROUP].reshape(rows, SSM_W).astype(BF16)
            for r in range(0, rows, step):
                s_ref[r:r + step, :] = _dot(u_bf[r:r + step], bmat_ref[d])
            for b in range(SSM_GROUP):
                carry = carry_ref[d, g0 + b:g0 + b + 1, :]
                carry = _tile_scan(s_ref, hb_ref, tab_ref, d, b * SSM_CHUNK, carry, d == 1)
                carry_ref[d, g0 + b:g0 + b + 1, :] = carry
            for r in range(0, rows, step):
                y = _dot(hb_ref[r:r + step, :], cmat_ref[d])
                for b in range(step // SSM_CHUNK):
                    y_ref[g0 + r // SSM_CHUNK + b] = y[b * SSM_CHUNK:(b + 1) * SSM_CHUNK]
    fin_ref[...] = carry_ref[...]


def _ssm(u_fwd_spec, u_bwd_spec, y_fwd_spec, y_bwd_spec, y_shape, grid, nseq, nseq_total, name):
    state = pl.BlockSpec((2, nseq, 2 * SSM_N), lambda g, c: (0, g, 0))
    rows = SSM_GROUP * SSM_CHUNK
    return pl.pallas_call(
        _ssm_kernel,
        grid=grid,
        in_specs=[u_fwd_spec, u_bwd_spec, _full((2, SSM_W, 2 * SSM_N)), _full((2, 2 * SSM_N, SSM_W)),
                  _full((2, 4, 8, 2 * SSM_N)), state],
        out_specs=[y_fwd_spec, y_bwd_spec, state],
        out_shape=[jax.ShapeDtypeStruct(y_shape, F32), jax.ShapeDtypeStruct(y_shape, F32),
                   jax.ShapeDtypeStruct((2, nseq_total, 2 * SSM_N), F32)],
        scratch_shapes=[pltpu.VMEM((rows, 2 * SSM_N), F32), pltpu.VMEM((rows, 2 * SSM_N), BF16),
                        pltpu.VMEM((2, nseq, 2 * SSM_N), F32)],
        compiler_params=_params(("parallel", "arbitrary")),
        name=name,
    )


def _ssm_ctx(u, bmat, cmat, tab, h0):
    nseq = 8
    blk = pl.BlockSpec((nseq, CTX_L, SSM_W), lambda g, c: (LAT_SEQ_BLOCKS // nseq + g, 0, 0))
    out = pl.BlockSpec((nseq, CTX_L, SSM_W), lambda g, c: (g, 0, 0))
    call = _ssm(blk, blk, out, out, (N_CTX_B, CTX_L, SSM_W), (N_CTX_B // nseq, 1), nseq, N_CTX_B, "ssm_ctx")
    u3 = u.reshape(T_ALL // CTX_L, CTX_L, SSM_W)
    return call(u3, u3, bmat, cmat, tab, h0)


def _ssm_lat(u, bmat, cmat, tab, h0):
    nc = LAT_L // SSM_CHUNK
    fwd = pl.BlockSpec((N_LAT_B, None, SSM_CHUNK, SSM_W), lambda g, c: (0, c, 0, 0))
    bwd = pl.BlockSpec((N_LAT_B, None, SSM_CHUNK, SSM_W), lambda g, c: (0, nc - 1 - c, 0, 0))
    call = _ssm(fwd, bwd, fwd, bwd, (N_LAT_B, nc, SSM_CHUNK, SSM_W), (1, nc), N_LAT_B, N_LAT_B, "ssm_lat")
    u4 = u.reshape(T_ALL // LAT_L, nc, SSM_CHUNK, SSM_W)
    return call(u4, u4, bmat, cmat, tab, h0)


def _gelu_tanh(x):
    return 0.5 * x * (1.0 + jnp.tanh(math.sqrt(2.0 / math.pi) * (x + 0.044715 * (x * x * x))))


def _outproj_kernel(x_ref, u_ref, oal_ref, oac_ref, ocl_ref, occ_ref, yfl_ref, yfc_ref, ybl_ref, ybc_ref,
                    mod_ref, d_ref, wglu_ref, woa_ref, wob_ref, woc_ref, gffn_ref, wrhi_ref, wrlo_ref, rb_ref,
                    x1_ref, h2_ref, comb_ref):
    is_lat = pl.program_id(0) < N_LAT_TILES
    oa = jnp.where(is_lat, oal_ref[...], oac_ref[...])
    oc = jnp.where(is_lat, ocl_ref[...], occ_ref[...])
    y = d_ref[...] * u_ref[...] + jnp.where(is_lat, yfl_ref[...] + ybl_ref[...], yfc_ref[...] + ybc_ref[...])
    g = _gelu_tanh(y)
    ob = g * jax.nn.sigmoid(_dot(g.astype(BF16), wglu_ref[...]))
    mix = _dot(oa, woa_ref[...]) + _dot(ob.astype(BF16), wob_ref[...]) + _dot(oc, woc_ref[...])
    x1 = x_ref[...] + mod_ref[2:3, :] * mix
    x1_ref[...] = x1
    ms = jnp.mean(x1 * x1, axis=-1, keepdims=True)
    h2 = x1 * lax.rsqrt(ms + EPS) * gffn_ref[...]
    h2 = h2 * (1.0 + mod_ref[4:5, :]) + mod_ref[3:4, :]
    h2_hi, h2_lo = _split_bf16(h2)
    h2_ref[...] = h2_hi

    lt = _dot_nt(wrhi_ref[...], h2_hi) + _dot_nt(wrhi_ref[...], h2_lo) + _dot_nt(wrlo_ref[...], h2_hi)
    sc = [jax.nn.sigmoid(lt[e:e + 1, :]) for e in range(N_EXP)]
    bi = [sc[e] + rb_ref[e:e + 1, 0:1] for e in range(N_EXP)]
    gs = []
    for gi in range(N_GRP):
        v = bi[gi * GRP_SZ:(gi + 1) * GRP_SZ]
        best2 = None
        for i in range(GRP_SZ):
            for j in range(i + 1, GRP_SZ):
                pair = v[i] + v[j]
                best2 = pair if best2 is None else jnp.maximum(best2, pair)
        gs.append(best2)
    best_g = jnp.zeros((1, TM), jnp.int32)
    best_v = gs[0]
    for gi in range(1, N_GRP):
        upd = gs[gi] > best_v
        best_g = jnp.where(upd, gi, best_g)
        best_v = jnp.where(upd, gs[gi], best_v)
    wts = []
    for e in range(N_EXP):
        gi = e // GRP_SZ
        beaten = jnp.zeros((1, TM), F32)
        for j in range(gi * GRP_SZ, (gi + 1) * GRP_SZ):
            if j == e:
                continue
            if j < e:
                beaten = beaten + jnp.where(bi[j] >= bi[e], 1.0, 0.0)
            else:
                beaten = beaten + jnp.where(bi[j] > bi[e], 1.0, 0.0)
        keep = jnp.where(best_g == gi, jnp.where(beaten < 1.5, 1.0, 0.0), 0.0)
        wts.append(keep * sc[e])
    tot = wts[0]
    for e in range(1, N_EXP):
        tot = tot + wts[e]
    inv = 1.0 / tot
    comb_t = jnp.concatenate([w_ * inv for w_ in wts] + [best_g.astype(F32)]
                             + [jnp.zeros((LANES - N_EXP - 1, TM), F32)], axis=0)
    comb_ref[...] = comb_t.T


def _outproj(x, u, oa_lat, oa_ctx, oc_lat, oc_ctx, yf_lat, yf_ctx, yb_lat, yb_ctx, mod_l, w, wr_hi, wr_lo, rb):
    nt = T_ALL // TM
    tile = lambda width: pl.BlockSpec((TM, width), lambda i: (i, 0))
    lat = lambda width: pl.BlockSpec((TM, width), lambda i: (jnp.minimum(i, N_LAT_TILES - 1), 0))
    ctx = lambda width: pl.BlockSpec((TM, width), lambda i: (jnp.maximum(i - N_LAT_TILES, 0), 0))
    in_specs = [
        tile(D), tile(256), lat(512), ctx(512), lat(256), ctx(256), lat(256), ctx(256), lat(256), ctx(256),
        pl.BlockSpec((None, 6, D), lambda i: (_mod_row(i), 0, 0)),
        _full((1, SSM_W)), _full((SSM_W, SSM_W)),
        _full((512, D)), _full((256, D)), _full((256, D)), _full((1, D)),
        _full((LANES, D)), _full((LANES, D)), _full((N_EXP, LANES)),
    ]
    return pl.pallas_call(
        _outproj_kernel,
        grid=(nt,),
        in_specs=in_specs,
        out_specs=[tile(D), tile(D), tile(LANES)],
        out_shape=[jax.ShapeDtypeStruct((T_ALL, D), F32), jax.ShapeDtypeStruct((T_ALL, D), BF16),
                   jax.ShapeDtypeStruct((T_ALL, LANES), F32)],
        compiler_params=_params(("parallel",)),
        name="outproj_router",
    )(x, u, oa_lat, oa_ctx, oc_lat, oc_ctx, yf_lat, yf_ctx, yb_lat, yb_ctx, mod_l, w["ssm_d"], w["wglu"],
      w["woa"], w["wob"], w["woc"], w["gffn"], wr_hi, wr_lo, rb)


def _moe_route(h_ref, comb_ref, xs_ref, ws_ref, ys_ref, pt_ref, seg_ref):
    lane = lax.broadcasted_iota(jnp.int32, (TM_MOE, LANES), 1)
    comb = comb_ref[...]
    grp = jnp.sum(jnp.where(lane == N_EXP, comb, 0.0), axis=-1, keepdims=True).astype(jnp.int32)
    onehot = jnp.where(lane == grp, 1.0, 0.0)
    tok_r = lax.broadcasted_iota(jnp.int32, (TM_MOE, TM_MOE), 0)
    tok_c = lax.broadcasted_iota(jnp.int32, (TM_MOE, TM_MOE), 1)
    earlier = jnp.where(tok_c < tok_r, 1.0, 0.0).astype(BF16)
    rank = _dot(earlier, onehot.astype(BF16))
    count = jnp.sum(onehot, axis=0, keepdims=True).astype(jnp.int32)
    padded = lax.shift_left(lax.shift_right_logical(count + (MOE_CHUNK - 1), MOE_SHIFT), MOE_SHIFT)
    lane1 = lax.broadcasted_iota(jnp.int32, (1, LANES), 1)
    base = jnp.zeros((1, LANES), jnp.int32)
    run = jnp.zeros((1, 1), jnp.int32)
    for g in range(N_GRP):
        base = jnp.where(lane1 == g, run, base)
        seg_ref[2 * g] = lax.shift_right_logical(run, MOE_SHIFT)[0, 0]
        seg_ref[2 * g + 1] = lax.shift_right_logical(padded[:, g:g + 1], MOE_SHIFT)[0, 0]
        run = run + padded[:, g:g + 1]
    slot = jnp.sum(onehot * (base.astype(F32) + rank), axis=-1, keepdims=True)
    slot_lane = lax.broadcasted_iota(jnp.int32, (TM_MOE, MOE_SLOTS), 1)
    pt_ref[...] = jnp.where(slot_lane == slot.astype(jnp.int32), 1.0, 0.0).astype(BF16)
    slot_row = jnp.broadcast_to(slot, (TM_MOE, LANES)).T[0:1, :].astype(jnp.int32)

    c1 = comb.astype(BF16).astype(F32)
    c2 = (comb - c1).astype(BF16).astype(F32)
    c3 = comb - c1 - c2
    pieces = (c1 + pltpu.roll(c2, 32, 1) + pltpu.roll(c3, 64, 1)).astype(BF16)
    x = h_ref[...]
    step = 512
    for r0 in range(0, MOE_SLOTS, step):
        slot_sub = lax.broadcasted_iota(jnp.int32, (step, TM_MOE), 0) + r0
        p = jnp.where(slot_sub == slot_row, 1.0, 0.0).astype(BF16)
        xs_ref[r0:r0 + step, :] = _dot(p, x).astype(BF16)
        wp = _dot(p, pieces)
        ws_ref[r0:r0 + step, :] = wp + pltpu.roll(wp, LANES - 32, 1) + pltpu.roll(wp, LANES - 64, 1)
    ys_ref[...] = jnp.zeros_like(ys_ref)


def _moe_kernel(h_ref, comb_ref, x1_ref, mod_ref, wg_ref, wu_ref, wd_ref, o_ref,
                xs_ref, ws_ref, ys_ref, pt_ref, seg_ref):
    e = pl.program_id(1)

    @pl.when(e == 0)
    def _():
        _moe_route(h_ref, comb_ref, xs_ref, ws_ref, ys_ref, pt_ref, seg_ref)

    g = e // GRP_SZ
    first = seg_ref[2 * g]
    lane = lax.broadcasted_iota(jnp.int32, (MOE_CHUNK, LANES), 1)

    def body(c, carry):
        r = pl.multiple_of((first + c) * MOE_CHUNK, MOE_CHUNK)
        xc = xs_ref[pl.ds(r, MOE_CHUNK), :]
        a = _dot(xc, wg_ref[...])
        b = _dot(xc, wu_ref[...])
        act = (a * jax.nn.sigmoid(a)) * b
        wcol = jnp.sum(jnp.where(lane == e, ws_ref[pl.ds(r, MOE_CHUNK), :], 0.0), axis=-1, keepdims=True)
        ys_ref[pl.ds(r, MOE_CHUNK), :] += wcol * _dot(act.astype(BF16), wd_ref[...])
        return carry

    lax.fori_loop(0, seg_ref[2 * g + 1], body, 0)

    @pl.when(e == N_EXP - 1)
    def _():
        ysb = ys_ref[...].astype(BF16)
        step = 256
        for r0 in range(0, TM_MOE, step):
            y = _dot(pt_ref[r0:r0 + step, :], ysb)
            o_ref[r0:r0 + step, :] = x1_ref[r0:r0 + step, :] + mod_ref[5:6, :] * y


def _moe(h2, comb, x1, mod_l, wg, wu, wd):
    nt = T_ALL // TM_MOE
    per = TM_MOE // TM
    tile = lambda width: pl.BlockSpec((TM_MOE, width), lambda i, e: (i, 0))
    return pl.pallas_call(
        _moe_kernel,
        grid=(nt, N_EXP),
        in_specs=[
            tile(D), tile(LANES), tile(D),
            pl.BlockSpec((None, 6, D), lambda i, e: (_mod_row(i * per), 0, 0)),
            pl.BlockSpec((None, D, D_EXP), lambda i, e: (e, 0, 0)),
            pl.BlockSpec((None, D, D_EXP), lambda i, e: (e, 0, 0)),
            pl.BlockSpec((None, D_EXP, D), lambda i, e: (e, 0, 0)),
        ],
        out_specs=tile(D),
        out_shape=jax.ShapeDtypeStruct((T_ALL, D), F32),
        scratch_shapes=[pltpu.VMEM((MOE_SLOTS, D), BF16), pltpu.VMEM((MOE_SLOTS, LANES), F32),
                        pltpu.VMEM((MOE_SLOTS, D), F32), pltpu.VMEM((TM_MOE, MOE_SLOTS), BF16),
                        pltpu.SMEM((2 * N_GRP,), jnp.int32)],
        compiler_params=pltpu.CompilerParams(dimension_semantics=("parallel", "arbitrary"),
                                             vmem_limit_bytes=MOE_VMEM_LIMIT),
        name="moe",
    )(h2, comb, x1, mod_l, wg, wu, wd)


def _pad_heads(w, heads, dim):
    k = w.shape[0]
    return jnp.pad(w.reshape(k, heads, dim), ((0, 0), (0, 0), (0, LANES - dim))).reshape(k, heads * LANES)


def _block_diag_b(bb_t):
    eye = jnp.eye(SSM_G, dtype=bb_t.dtype)
    return jnp.einsum("ghp,gk->ghkp", bb_t, eye).reshape(SSM_W, SSM_N)


def _block_diag_c(c):
    eye = jnp.eye(SSM_G, dtype=c.dtype)
    return jnp.einsum("ghp,gk->gpkh", c, eye).reshape(SSM_N, SSM_W)


def _scan_tables(pw_re, pw_im):
    def flat(p):
        return jnp.transpose(p, (0, 2, 1, 3)).reshape(2, 8, SSM_N)

    pw = jnp.concatenate([flat(pw_re), flat(pw_im)], axis=-1)
    pos = jnp.arange(8)[:, None]
    tabs = []
    for d in range(2):
        rows = []
        for k in (1, 2, 4):
            keep = (pos >= k) if d == 0 else (pos <= 7 - k)
            rows.append(jnp.where(keep, pw[d, k - 1][None, :], 0.0))
        rows.append(pw[d] if d == 0 else pw[d, ::-1])
        tabs.append(jnp.stack(rows))
    return jnp.stack(tabs)


def kernel(x_prompt, x_sample, cache_attn_k, cache_attn_v, cache_mla_ckv, cache_mla_kpe, state_ssm_re,
           state_ssm_im, c, c_ctx, norm_mix, norm_ffn, w_ada, b_ada, w_in, a_q_norm, a_k_norm, a_sink,
           ssm_a_re, ssm_a_im, ssm_log_dt, ssm_b_re, ssm_b_im, ssm_c_re, ssm_c_im, ssm_d, w_glu,
           mla_q_norm, mla_kv_norm, w_q_b, w_kv_b, mla_qh_norm, mla_kh_norm, w_out, w_router,
           router_bias, w_gate, w_up, w_down):
    x = jnp.concatenate([x_sample.reshape(T_LAT, D), x_prompt.reshape(T_CTX, D)], axis=0)

    cond = jnp.concatenate([c_ctx[None, :], c, jnp.zeros((16 - 1 - N_LAT_B, D), F32)], axis=0)
    mod = _adaln(cond, w_ada, b_ada)

    pw_re, pw_im, bbt_re, bbt_im = _discretize(ssm_a_re, ssm_a_im, ssm_log_dt, ssm_b_re, ssm_b_im)
    pw_re = pw_re.reshape(DEPTH, 2, SSM_G, 8, SSM_P)
    pw_im = pw_im.reshape(DEPTH, 2, SSM_G, 8, SSM_P)
    bbt_re = bbt_re.reshape(DEPTH, 2, SSM_G, SSM_H, SSM_P)
    bbt_im = bbt_im.reshape(DEPTH, 2, SSM_G, SSM_H, SSM_P)
    h0_ctx = jnp.zeros((2, N_CTX_B, 2 * SSM_N), F32)

    tab_a, tab_c = _rope_tables()

    wr_t = jnp.pad(w_router.T, ((0, LANES - N_EXP), (0, 0)))
    wr_hi = wr_t.astype(BF16)
    wr_lo = (wr_t - wr_hi.astype(F32)).astype(BF16)
    rb = jnp.broadcast_to(router_bias[:, None], (N_EXP, LANES))

    w_kv4 = w_kv_b.reshape(DEPTH, C_KVLORA, C_HEADS, C_NOPE + C_V)
    wkn_all = jnp.pad(w_kv4[..., :C_NOPE], ((0, 0), (0, 0), (0, 0), (0, LANES - C_NOPE))
                      ).reshape(DEPTH, C_KVLORA, C_HEADS * LANES).astype(BF16)
    wv_all = w_kv4[..., C_NOPE:].reshape(DEPTH, C_KVLORA, C_HEADS * C_V).astype(BF16)
    gkh_all = jnp.pad(mla_kh_norm, ((0, 0), (0, LANES - C_QK))).reshape(DEPTH, 1, LANES)

    kpe_slab = jnp.pad(cache_mla_kpe, ((0, 0), (0, 0), (0, 0), (C_NOPE, LANES - C_QK)))
    kx_all, vx_all = _cache_kv(cache_mla_ckv, kpe_slab, wkn_all, wv_all, gkh_all)

    cache_k = cache_attn_k.reshape(N_LAT_B, DEPTH, CTX_L, A_KV * A_DH)
    cache_v = cache_attn_v.reshape(N_LAT_B, DEPTH, CTX_L, A_KV * A_DH)

    ks, vs, ckvs, kpes, fins = [], [], [], [], []
    for l in range(DEPTH):
        wl = w_in[l]
        w = {
            "gmix": norm_mix[l][None, :],
            "wq": wl[:, 0:512].astype(BF16),
            "wkv": wl[:, 512:768].astype(BF16),
            "wu": wl[:, 768:1024].astype(BF16),
            "wcq": wl[:, 1024:1280].astype(BF16),
            "wt": jnp.pad(wl[:, 1280:1440], ((0, 0), (0, 256 - 160))).astype(BF16),
            "gq": jnp.tile(a_q_norm[l], 2)[None, :],
            "gk": jnp.tile(a_k_norm[l], 2)[None, :],
            "gcq": mla_q_norm[l][None, :],
            "gckv": mla_kv_norm[l][None, :],
            "wqb": _pad_heads(w_q_b[l], C_HEADS, C_QK).astype(BF16),
            "wkn": wkn_all[l],
            "wv": wv_all[l],
            "gqh": jnp.pad(mla_qh_norm[l], (0, LANES - C_QK))[None, :],
            "gkh": gkh_all[l],
            "ssm_d": ssm_d[l].reshape(1, SSM_W),
            "wglu": w_glu[l].astype(BF16),
            "woa": w_out[l, 0:512].astype(BF16),
            "wob": w_out[l, 512:768].astype(BF16),
            "woc": w_out[l, 768:1024].astype(BF16),
            "gffn": norm_ffn[l][None, :],
        }
        mod_l = mod[l]
        qa, ka, va, vaug, u, qc, kc, vc, ckv_n, kpe = _proj(x, mod_l, w, tab_a, tab_c)

        sink = jnp.broadcast_to(a_sink[l][:, None], (A_HEADS, LANES))
        oa_ctx = _attn_a_ctx(qa, ka, vaug, sink)
        oa_lat = _attn_a_lat(qa, ka, vaug, cache_k[:, l], cache_v[:, l], sink)
        oc_ctx = _mla_ctx(qc, kc, vc)
        oc_lat = _mla_lat(qc, kc, vc, kx_all[l], vx_all[l])

        bmat = jnp.stack([jnp.concatenate([_block_diag_b(bbt_re[l, d]), _block_diag_b(bbt_im[l, d])], axis=1)
                          for d in range(2)]).astype(BF16)
        cmat = jnp.stack([jnp.concatenate([_block_diag_c(ssm_c_re[l, d]), -_block_diag_c(ssm_c_im[l, d])], axis=0)
                          for d in range(2)]).astype(BF16)
        tab = _scan_tables(pw_re[l], pw_im[l])
        yf_ctx, yb_ctx, fin = _ssm_ctx(u, bmat, cmat, tab, h0_ctx)
        h0 = jnp.swapaxes(jnp.concatenate([state_ssm_re[:, l].reshape(N_LAT_B, 2, SSM_N),
                                            state_ssm_im[:, l].reshape(N_LAT_B, 2, SSM_N)], axis=-1), 0, 1)
        yf_lat, yb_lat, _ = _ssm_lat(u, bmat, cmat, tab, h0)

        x1, h2, comb = _outproj(x, u, oa_lat, oa_ctx, oc_lat, oc_ctx,
                                yf_lat.reshape(T_LAT, SSM_W), yf_ctx.reshape(T_CTX, SSM_W),
                                yb_lat.reshape(T_LAT, SSM_W), yb_ctx.reshape(T_CTX, SSM_W),
                                mod_l, w, wr_hi, wr_lo, rb)
        x = _moe(h2, comb, x1, mod_l, w_gate[l].astype(BF16), w_up[l].astype(BF16), w_down[l].astype(BF16))

        ks.append(ka[T_LAT:].reshape(N_CTX_B, CTX_L, A_KV, A_DH))
        vs.append(va[T_LAT:].reshape(N_CTX_B, CTX_L, A_KV, A_DH))
        ckvs.append(ckv_n[T_LAT:].reshape(N_CTX_B, CTX_L, C_KVLORA))
        kpes.append(kpe[T_LAT:].reshape(N_CTX_B, CTX_L, C_ROPE))
        fins.append(fin)

    y_prompt = x[T_LAT:].reshape(N_CTX_B, CTX_L, D)
    y_sample = x[:T_LAT].reshape(N_LAT_B, LAT_L, D)
    fin = jnp.transpose(jnp.stack(fins, axis=0), (2, 0, 1, 3))
    new_re = fin[..., 0:SSM_N].reshape(N_CTX_B, DEPTH, 2, SSM_G, SSM_P)
    new_im = fin[..., SSM_N:].reshape(N_CTX_B, DEPTH, 2, SSM_G, SSM_P)
    return (y_prompt, y_sample, jnp.stack(ks, axis=1), jnp.stack(vs, axis=1), jnp.stack(ckvs, axis=1),
            jnp.stack(kpes, axis=1), new_re, new_im)
```

```python
import functools
import math

import jax
import jax.numpy as jnp
from jax import lax
from jax.experimental import pallas as pl
from jax.experimental.pallas import tpu as pltpu

F32 = jnp.float32
BF16 = jnp.bfloat16

D = 1024
N_CTX_B, CTX_L = 32, 256
N_LAT_B, LAT_L = 4, 4096
DEPTH = 4
GRID_W = 64
EPS = 1e-6
ROPE_BASE = 10000.0
A_HEADS, A_KV, A_DH = 8, 2, 64
SSM_G, SSM_H, SSM_P = 16, 16, 64
SSM_W = SSM_G * SSM_H
SSM_N = SSM_G * SSM_P
SSM_SLABS = SSM_N // 128
C_HEADS, C_NOPE, C_ROPE, C_V = 4, 64, 32, 64
C_QK = C_NOPE + C_ROPE
C_QLORA, C_KVLORA = 256, 128
N_EXP, N_GRP, GRP_SZ, D_EXP = 16, 4, 4, 512
WINDOW = 128

T_CTX = N_CTX_B * CTX_L
T_LAT = N_LAT_B * LAT_L
T_ALL = T_CTX + T_LAT

LANES = 128
HALF = LANES // 2
VMEM_LIMIT = 48 * 1024 * 1024

TM = 512
N_LAT_TILES = T_LAT // TM
LAT_TILES_PER_B = LAT_L // TM
LAT_SEQ_BLOCKS = T_LAT // CTX_L
TQ_A = 256
TQ_C = 256
CTX_ATTN_SEQS = 4
TM_MOE = 1024
MOE_SHIFT = 7
MOE_CHUNK = 1 << MOE_SHIFT
MOE_SLOTS = TM_MOE + N_GRP * MOE_CHUNK
MOE_VMEM_LIMIT = 56 * 1024 * 1024
SSM_CHUNK = 256
SSM_GROUP = 4
NEG = -1e30
LOG2E = math.log2(math.e)


def _dot(a, b):
    return jnp.dot(a, b, preferred_element_type=F32)


def _dot_nt(a, b):
    return lax.dot_general(a, b, (((1,), (1,)), ((), ())), preferred_element_type=F32)


def _split_bf16(x):
    hi = x.astype(BF16)
    lo = (x - hi.astype(F32)).astype(BF16)
    return hi, lo


def _params(sem):
    return pltpu.CompilerParams(dimension_semantics=sem, vmem_limit_bytes=VMEM_LIMIT)


def _full(shape):
    n = len(shape)
    return pl.BlockSpec(shape, lambda *_: (0,) * n)


def _mod_row(i):
    return jnp.where(i < N_LAT_TILES, 1 + i // LAT_TILES_PER_B, 0)


def _pos_block(i):
    return jnp.where(i < N_LAT_TILES, i % LAT_TILES_PER_B, LAT_TILES_PER_B)


def _adaln_kernel(c_ref, w_ref, b_ref, o_ref):
    c = c_ref[...]
    s = c * jax.nn.sigmoid(c)
    s_hi, s_lo = _split_bf16(s)
    w_hi, w_lo = _split_bf16(w_ref[...])
    o_ref[...] = _dot(s_hi, w_hi) + _dot(s_hi, w_lo) + _dot(s_lo, w_hi) + b_ref[...]


def _adaln(cond, w_ada, b_ada):
    nt = 6
    out = pl.pallas_call(
        _adaln_kernel,
        grid=(DEPTH, nt),
        in_specs=[
            pl.BlockSpec((16, D), lambda l, n: (0, 0)),
            pl.BlockSpec((None, D, D), lambda l, n: (l, 0, n)),
            pl.BlockSpec((None, 1, D), lambda l, n: (l, 0, n)),
        ],
        out_specs=pl.BlockSpec((None, 16, D), lambda l, n: (l, 0, n)),
        out_shape=jax.ShapeDtypeStruct((DEPTH, 16, nt * D), F32),
        compiler_params=_params(("parallel", "parallel")),
        name="adaln",
    )(cond, w_ada, b_ada.reshape(DEPTH, 1, nt * D))
    return out.reshape(DEPTH, 16, nt, D)


def _disc_kernel(are_ref, aim_ref, ldt_ref, bre_ref, bim_ref, pwre_ref, pwim_ref, bbre_ref, bbim_ref):
    a_re = are_ref[...]
    a_im = aim_ref[...]
    dt = jnp.exp(ldt_ref[...])
    mag = jnp.exp(a_re * dt)
    ab_re = mag * jnp.cos(a_im * dt)
    ab_im = mag * jnp.sin(a_im * dt)
    den = a_re * a_re + a_im * a_im
    n_re = ab_re - 1.0
    k_re = (n_re * a_re + ab_im * a_im) / den
    k_im = (ab_im * a_re - n_re * a_im) / den
    b_re = bre_ref[...]
    b_im = bim_ref[...]
    bbre_ref[...] = k_re * b_re - k_im * b_im
    bbim_ref[...] = k_re * b_im + k_im * b_re
    p_re, p_im = ab_re, ab_im
    for k in range(8):
        pwre_ref[:, k:k + 1, :] = p_re
        pwim_ref[:, k:k + 1, :] = p_im
        p_re, p_im = p_re * ab_re - p_im * ab_im, p_re * ab_im + p_im * ab_re


def _discretize(ssm_a_re, ssm_a_im, ssm_log_dt, ssm_b_re, ssm_b_im):
    n = DEPTH * 2 * SSM_G
    a_re = ssm_a_re.reshape(n, 1, SSM_P)
    a_im = ssm_a_im.reshape(n, 1, SSM_P)
    ldt = jnp.broadcast_to(ssm_log_dt.reshape(n, 1, 1), (n, 1, SSM_P))
    b_re = jnp.swapaxes(ssm_b_re.reshape(n, SSM_P, SSM_H), 1, 2)
    b_im = jnp.swapaxes(ssm_b_im.reshape(n, SSM_P, SSM_H), 1, 2)
    sa = jax.ShapeDtypeStruct((n, 8, SSM_P), F32)
    sb = jax.ShapeDtypeStruct((n, SSM_H, SSM_P), F32)
    return pl.pallas_call(
        _disc_kernel,
        out_shape=(sa, sa, sb, sb),
        compiler_params=pltpu.CompilerParams(vmem_limit_bytes=VMEM_LIMIT),
        name="s5_discretize",
    )(a_re, a_im, ldt, b_re, b_im)


def _rope_tables():
    t = jnp.arange(LAT_L)
    row = (t // GRID_W).astype(F32)[:, None]
    col = (t % GRID_W).astype(F32)[:, None]
    lane = jnp.arange(LANES)

    def build(d, width, valid):
        half = width // 2
        q = half // 2
        on_col = d >= half
        dd = jnp.where(on_col, d - half, d)
        first = dd < q
        j = jnp.where(first, dd, dd - q).astype(F32)
        inv = ROPE_BASE ** (-j / q)
        ang = jnp.where(on_col[None, :], col, row) * inv[None, :]
        cos = jnp.where(valid[None, :], jnp.cos(ang), 1.0)
        sin = jnp.where(valid[None, :], jnp.sin(ang), 0.0)
        s_up = jnp.where(first[None, :], -sin, 0.0)
        s_dn = jnp.where(first[None, :], 0.0, sin)
        tab = jnp.concatenate([cos, s_up, s_dn], axis=1)
        ident = jnp.concatenate([jnp.ones((TM, LANES), F32), jnp.zeros((TM, 2 * LANES), F32)], axis=1)
        return jnp.concatenate([tab, ident], axis=0)

    tab_a = build(lane % A_DH, A_DH, jnp.ones((LANES,), bool))
    dc = jnp.clip(lane - C_NOPE, 0, C_ROPE - 1)
    tab_c = build(dc, C_ROPE, (lane >= C_NOPE) & (lane < C_QK))
    return tab_a, tab_c


def _rope(x, tab_ref, shift):
    cos = tab_ref[:, 0:LANES]
    s_up = tab_ref[:, LANES:2 * LANES]
    s_dn = tab_ref[:, 2 * LANES:3 * LANES]
    return x * cos + pltpu.roll(x, LANES - shift, 1) * s_up + pltpu.roll(x, shift, 1) * s_dn


def _half_norm(xs, gain):
    lo = lax.broadcasted_iota(jnp.int32, xs.shape, 1) < A_DH
    sq = xs * xs
    s_lo = jnp.sum(jnp.where(lo, sq, 0.0), axis=-1, keepdims=True)
    s_hi = jnp.sum(jnp.where(lo, 0.0, sq), axis=-1, keepdims=True)
    r = jnp.where(lo, lax.rsqrt(s_lo * (1.0 / A_DH) + EPS), lax.rsqrt(s_hi * (1.0 / A_DH) + EPS))
    return xs * r * gain


def _with_ones(v, heads):
    lo = lax.broadcasted_iota(jnp.int32, (v.shape[0], LANES), 1) < HALF
    slabs = []
    for h in range(heads):
        pair = v[:, (h // 2) * LANES:(h // 2 + 1) * LANES]
        own = pair if h % 2 == 0 else pltpu.roll(pair, HALF, 1)
        slabs.append(jnp.where(lo, own, 1.0).astype(BF16))
    return slabs


def _normalise(o):
    return o / pltpu.roll(o, HALF, 1)


def _store_head_pairs(rs, o_ref):
    lo = lax.broadcasted_iota(jnp.int32, rs[0].shape, 1) < HALF
    for j in range(len(rs) // 2):
        slab = jnp.where(lo, rs[2 * j], pltpu.roll(rs[2 * j + 1], HALF, 1))
        o_ref[:, j * LANES:(j + 1) * LANES] = slab.astype(o_ref.dtype)


def _mla_keys_values(ckv_n, kpe_slab, wkn_ref, wv_ref, gkh, tab_ref):
    cb = ckv_n.astype(BF16)
    kn = _dot(cb, wkn_ref[...])
    v = _dot(cb, wv_ref[...])
    ks = []
    for h in range(C_HEADS):
        slab = kn[:, h * LANES:(h + 1) * LANES] + kpe_slab
        ss = jnp.sum(slab * slab, axis=-1, keepdims=True)
        slab = slab * lax.rsqrt(ss * (1.0 / C_QK) + EPS) * gkh
        if tab_ref is not None:
            slab = _rope(slab, tab_ref, C_ROPE // 4)
        ks.append(slab.astype(BF16))
    return ks, v


def _proj_kernel(x_ref, mod_ref, gmix_ref, wq_ref, wkv_ref, wu_ref, wcq_ref, wt_ref,
                 gq_ref, gk_ref, gcq_ref, gckv_ref, wqb_ref, wkn_ref, wv_ref, gqh_ref, gkh_ref,
                 ta_ref, tc_ref,
                 qa_ref, ka_ref, va_ref, vaug_ref, u_ref, qc_ref, kc_ref, vc_ref, ckv_ref, kpe_ref):
    x = x_ref[...]
    sh = mod_ref[0:1, :]
    sc = mod_ref[1:2, :]
    ms = jnp.mean(x * x, axis=-1, keepdims=True)
    h = x * lax.rsqrt(ms + EPS) * gmix_ref[...]
    h = (h * (1.0 + sc) + sh).astype(BF16)

    lo = lax.broadcasted_iota(jnp.int32, (TM, LANES), 1) < A_DH

    p_q = _dot(h, wq_ref[...])
    for j in range(A_HEADS // 2):
        slab = _half_norm(p_q[:, j * LANES:(j + 1) * LANES], gq_ref[...])
        slab = _rope(slab, ta_ref, A_DH // 4) * (A_DH ** -0.5 * LOG2E)
        swapped = pltpu.roll(slab, A_DH, 1)
        kv_head = (2 * j) // (A_HEADS // A_KV)
        if kv_head == 0:
            q0 = jnp.where(lo, slab, 0.0)
            q1 = jnp.where(lo, swapped, 0.0)
        else:
            q0 = jnp.where(lo, 0.0, swapped)
            q1 = jnp.where(lo, 0.0, slab)
        qa_ref[:, (2 * j) * LANES:(2 * j + 1) * LANES] = q0.astype(BF16)
        qa_ref[:, (2 * j + 1) * LANES:(2 * j + 2) * LANES] = q1.astype(BF16)

    p_kv = _dot(h, wkv_ref[...])
    k = _half_norm(p_kv[:, 0:LANES], gk_ref[...])
    ka_ref[...] = _rope(k, ta_ref, A_DH // 4)
    va_ref[...] = p_kv[:, LANES:2 * LANES]
    for kvh, slab in enumerate(_with_ones(p_kv[:, LANES:2 * LANES], A_KV)):
        vaug_ref[:, kvh * LANES:(kvh + 1) * LANES] = slab

    u_ref[...] = _dot(h, wu_ref[...])

    p_cq = _dot(h, wcq_ref[...])
    ms = jnp.mean(p_cq * p_cq, axis=-1, keepdims=True)
    cq = (p_cq * lax.rsqrt(ms + EPS) * gcq_ref[...]).astype(BF16)
    q = _dot(cq, wqb_ref[...])
    for hd in range(C_HEADS):
        slab = q[:, hd * LANES:(hd + 1) * LANES]
        ss = jnp.sum(slab * slab, axis=-1, keepdims=True)
        slab = slab * lax.rsqrt(ss * (1.0 / C_QK) + EPS) * gqh_ref[...]
        slab = _rope(slab, tc_ref, C_ROPE // 4) * (C_QK ** -0.5 * LOG2E)
        qc_ref[:, hd * LANES:(hd + 1) * LANES] = slab.astype(BF16)

    p_t = _dot(h, wt_ref[...])
    ckv = p_t[:, 0:LANES]
    ms = jnp.mean(ckv * ckv, axis=-1, keepdims=True)
    ckv_n = ckv * lax.rsqrt(ms + EPS) * gckv_ref[...]
    ckv_ref[...] = ckv_n
    kpe_wide = p_t[:, LANES:2 * LANES]
    kpe_ref[...] = kpe_wide[:, 0:C_ROPE]
    kpe_slab = pltpu.roll(kpe_wide, C_NOPE, 1)
    ks, v = _mla_keys_values(ckv_n, kpe_slab, wkn_ref, wv_ref, gkh_ref[...], tc_ref)
    for hd, slab in enumerate(_with_ones(v, C_HEADS)):
        kc_ref[:, hd * LANES:(hd + 1) * LANES] = ks[hd]
        vc_ref[:, hd * LANES:(hd + 1) * LANES] = slab


def _proj(x, mod_l, w, tab_a, tab_c):
    nt = T_ALL // TM
    tile = lambda width: pl.BlockSpec((TM, width), lambda i: (i, 0))
    tab = pl.BlockSpec((TM, 3 * LANES), lambda i: (_pos_block(i), 0))
    in_specs = [
        tile(D),
        pl.BlockSpec((None, 6, D), lambda i: (_mod_row(i), 0, 0)),
        _full((1, D)),
        _full((D, 512)), _full((D, 256)), _full((D, 256)), _full((D, 256)), _full((D, 256)),
        _full((1, LANES)), _full((1, LANES)), _full((1, 256)), _full((1, LANES)),
        _full((256, 512)), _full((LANES, 512)), _full((LANES, 256)),
        _full((1, LANES)), _full((1, LANES)),
        tab, tab,
    ]
    outs = [(1024, BF16), (128, F32), (128, F32), (256, BF16), (256, F32), (512, BF16), (512, BF16), (512, BF16),
            (128, F32), (C_ROPE, F32)]
    return pl.pallas_call(
        _proj_kernel,
        grid=(nt,),
        in_specs=in_specs,
        out_specs=[tile(wd) for wd, _ in outs],
        out_shape=[jax.ShapeDtypeStruct((T_ALL, wd), dt) for wd, dt in outs],
        compiler_params=_params(("parallel",)),
        name="proj",
    )(x, mod_l, w["gmix"], w["wq"], w["wkv"], w["wu"], w["wcq"], w["wt"],
      w["gq"], w["gk"], w["gcq"], w["gckv"], w["wqb"], w["wkn"], w["wv"], w["gqh"], w["gkh"],
      tab_a, tab_c)


def _cache_kv_kernel(ckv_ref, kpe_ref, wkn_ref, wv_ref, gkh_ref, k_ref, v_ref):
    ks, v = _mla_keys_values(ckv_ref[...], kpe_ref[...], wkn_ref, wv_ref, gkh_ref[...], None)
    for hd, slab in enumerate(_with_ones(v, C_HEADS)):
        k_ref[:, hd * LANES:(hd + 1) * LANES] = ks[hd]
        v_ref[:, hd * LANES:(hd + 1) * LANES] = slab


def _cache_kv(cache_ckv, cache_kpe_slab, wkn, wv, gkh):
    return pl.pallas_call(
        _cache_kv_kernel,
        grid=(DEPTH, N_LAT_B),
        in_specs=[
            pl.BlockSpec((None, None, CTX_L, LANES), lambda l, b: (b, l, 0, 0)),
            pl.BlockSpec((None, None, CTX_L, LANES), lambda l, b: (b, l, 0, 0)),
            pl.BlockSpec((None, LANES, 512), lambda l, b: (l, 0, 0)),
            pl.BlockSpec((None, LANES, 256), lambda l, b: (l, 0, 0)),
            pl.BlockSpec((None, 1, LANES), lambda l, b: (l, 0, 0)),
        ],
        out_specs=[
            pl.BlockSpec((None, None, CTX_L, 512), lambda l, b: (l, b, 0, 0)),
            pl.BlockSpec((None, None, CTX_L, 512), lambda l, b: (l, b, 0, 0)),
        ],
        out_shape=[jax.ShapeDtypeStruct((DEPTH, N_LAT_B, CTX_L, 512), BF16),
                   jax.ShapeDtypeStruct((DEPTH, N_LAT_B, CTX_L, 512), BF16)],
        compiler_params=_params(("parallel", "parallel")),
        name="mla_cache_kv",
    )(cache_ckv, cache_kpe_slab, wkn, wv, gkh)


def _sink_column(sink_ref, kv_head, rows):
    grp = A_HEADS // A_KV
    return jnp.concatenate(
        [jnp.broadcast_to(sink_ref[kv_head * grp + g:kv_head * grp + g + 1, 0:1] * LOG2E, (rows, 1))
         for g in range(grp)],
        axis=0)


def _sink_softmax_head(q_ref, kv_head, rows, kb, vslab, valid, sink_ref):
    grp = A_HEADS // A_KV
    q = jnp.concatenate([q_ref[:, (kv_head * grp + g) * LANES:(kv_head * grp + g + 1) * LANES] for g in range(grp)],
                        axis=0)
    s = _dot_nt(q, kb)
    if valid is not None:
        s = jnp.where(valid, s, NEG)
    sk = _sink_column(sink_ref, kv_head, rows)
    m = jnp.maximum(jnp.max(s, axis=-1, keepdims=True), sk)
    o = _dot(jnp.exp2(s - m).astype(BF16), vslab)
    lo = lax.broadcasted_iota(jnp.int32, o.shape, 1) < HALF
    o = o + jnp.where(lo, 0.0, jnp.exp2(sk - m))
    r = _normalise(o)
    return [r[g * rows:(g + 1) * rows] for g in range(grp)]


def _attn_a_ctx_kernel(q_ref, k_ref, v_ref, sink_ref, o_ref):
    for i in range(CTX_ATTN_SEQS):
        kb = k_ref[i].astype(BF16)
        rs = []
        for kvh in range(A_KV):
            rs.extend(_sink_softmax_head(q_ref.at[i], kvh, CTX_L, kb, v_ref[i, :, kvh * LANES:(kvh + 1) * LANES],
                                         None, sink_ref))
        _store_head_pairs(rs, o_ref.at[i])


def _ctx_seq_spec(width, nseq=CTX_ATTN_SEQS):
    return pl.BlockSpec((nseq, CTX_L, width), lambda g: (LAT_SEQ_BLOCKS // nseq + g, 0, 0))


def _by_seq(a):
    return a.reshape(T_ALL // CTX_L, CTX_L, a.shape[-1])


def _attn_a_ctx(qa, ka, va, sink):
    out = pl.pallas_call(
        _attn_a_ctx_kernel,
        grid=(N_CTX_B // CTX_ATTN_SEQS,),
        in_specs=[_ctx_seq_spec(1024), _ctx_seq_spec(LANES), _ctx_seq_spec(A_KV * LANES), _full((A_HEADS, LANES))],
        out_specs=pl.BlockSpec((CTX_ATTN_SEQS, CTX_L, 512), lambda g: (g, 0, 0)),
        out_shape=jax.ShapeDtypeStruct((N_CTX_B, CTX_L, 512), BF16),
        compiler_params=_params(("parallel",)),
        name="attn_a_ctx",
    )(_by_seq(qa), _by_seq(ka), _by_seq(va), sink)
    return out.reshape(T_CTX, 512)


def _attn_a_lat_kernel(q_ref, kp_ref, kc_ref, kn_ref, vp_ref, vc_ref, vn_ref, kx_ref, vx_ref, sink_ref, o_ref):
    grp = A_HEADS // A_KV
    n = pl.program_id(1)
    nb = LAT_L // TQ_A
    kb = jnp.concatenate([kp_ref[...], kc_ref[...], kn_ref[...], kx_ref[...]], axis=0).astype(BF16)
    vx = _with_ones(vx_ref[...], A_KV)
    rows = grp * TQ_A
    nkeys = TQ_A + 2 * WINDOW + CTX_L
    qi = lax.broadcasted_iota(jnp.int32, (rows, nkeys), 0) % TQ_A
    kj = lax.broadcasted_iota(jnp.int32, (rows, nkeys), 1)
    rel = kj - WINDOW - qi
    valid = (jnp.abs(rel) <= WINDOW)
    valid = valid & ((kj >= WINDOW) | (n > 0)) & ((kj < WINDOW + TQ_A) | (n < nb - 1))
    valid = valid | (kj >= TQ_A + 2 * WINDOW)
    rs = []
    for kvh in range(A_KV):
        hs = slice(kvh * LANES, (kvh + 1) * LANES)
        vslab = jnp.concatenate([vp_ref[:, hs], vc_ref[:, hs], vn_ref[:, hs], vx[kvh]], axis=0)
        rs.extend(_sink_softmax_head(q_ref, kvh, TQ_A, kb, vslab, valid, sink_ref))
    _store_head_pairs(rs, o_ref)


def _attn_a_lat(qa, ka, va, cache_k_l, cache_v_l, sink):
    nb = LAT_L // TQ_A
    base = 0

    per = TQ_A // WINDOW
    nw = LAT_L // WINDOW
    own = lambda width: pl.BlockSpec((TQ_A, width), lambda b, n: (base + b * nb + n, 0))
    prev = lambda width: pl.BlockSpec((WINDOW, width), lambda b, n: (b * nw + jnp.maximum(n * per - 1, 0), 0))
    nxt = lambda width: pl.BlockSpec((WINDOW, width), lambda b, n: (b * nw + jnp.minimum((n + 1) * per, nw - 1), 0))
    cache = pl.BlockSpec((None, CTX_L, LANES), lambda b, n: (b, 0, 0))
    vw = A_KV * LANES
    return pl.pallas_call(
        _attn_a_lat_kernel,
        grid=(N_LAT_B, nb),
        in_specs=[
            own(1024),
            prev(LANES), own(LANES), nxt(LANES), prev(vw), own(vw), nxt(vw),
            cache, cache, _full((A_HEADS, LANES)),
        ],
        out_specs=pl.BlockSpec((TQ_A, 512), lambda b, n: (b * nb + n, 0)),
        out_shape=jax.ShapeDtypeStruct((T_LAT, 512), BF16),
        compiler_params=_params(("parallel", "parallel")),
        name="attn_a_lat",
    )(qa, ka, ka, ka, va, va, va, cache_k_l, cache_v_l, sink)


def _mla_ctx_kernel(q_ref, k_ref, v_ref, o_ref):
    for i in range(q_ref.shape[0]):
        rs = []
        for h in range(C_HEADS):
            hs = slice(h * LANES, (h + 1) * LANES)
            s = _dot_nt(q_ref[i, :, hs], k_ref[i, :, hs])
            m = jnp.max(s, axis=-1, keepdims=True)
            rs.append(_normalise(_dot(jnp.exp2(s - m).astype(BF16), v_ref[i, :, hs])))
        _store_head_pairs(rs, o_ref.at[i])


def _mla_ctx(qc, kc, vc):
    out = pl.pallas_call(
        _mla_ctx_kernel,
        grid=(N_CTX_B,),
        in_specs=[_ctx_seq_spec(512, 1), _ctx_seq_spec(512, 1), _ctx_seq_spec(512, 1)],
        out_specs=pl.BlockSpec((1, CTX_L, 256), lambda g: (g, 0, 0)),
        out_shape=jax.ShapeDtypeStruct((N_CTX_B, CTX_L, 256), BF16),
        compiler_params=_params(("parallel",)),
        name="mla_ctx",
    )(_by_seq(qc), _by_seq(kc), _by_seq(vc))
    return out.reshape(T_CTX, 256)


def _mla_lat_kernel(q_ref, k_ref, v_ref, kx_ref, vx_ref, o_ref):
    rs = []
    for h in range(C_HEADS):
        hs = slice(h * LANES, (h + 1) * LANES)
        q = q_ref[:, hs]
        s_lat = _dot_nt(q, k_ref[:, hs])
        s_ctx = _dot_nt(q, kx_ref[:, hs])
        m = jnp.max(jnp.concatenate([s_lat, s_ctx], axis=1), axis=-1, keepdims=True)
        o = (_dot(jnp.exp2(s_lat - m).astype(BF16), v_ref[:, hs])
             + _dot(jnp.exp2(s_ctx - m).astype(BF16), vx_ref[:, hs]))
        rs.append(_normalise(o))
    _store_head_pairs(rs, o_ref)


def _mla_lat(qc, kc, vc, kx_l, vx_l):
    nq = LAT_L // TQ_C
    qbase = 0
    kbase = 0
    return pl.pallas_call(
        _mla_lat_kernel,
        grid=(N_LAT_B, nq),
        in_specs=[
            pl.BlockSpec((TQ_C, 512), lambda b, n: (qbase + b * nq + n, 0)),
            pl.BlockSpec((LAT_L, 512), lambda b, n: (kbase + b, 0)),
            pl.BlockSpec((LAT_L, 512), lambda b, n: (kbase + b, 0)),
            pl.BlockSpec((None, CTX_L, 512), lambda b, n: (b, 0, 0)),
            pl.BlockSpec((None, CTX_L, 512), lambda b, n: (b, 0, 0)),
        ],
        out_specs=pl.BlockSpec((TQ_C, 256), lambda b, n: (b * nq + n, 0)),
        out_shape=jax.ShapeDtypeStruct((T_LAT, 256), BF16),
        compiler_params=_params(("parallel", "arbitrary")),
        name="mla_lat",
    )(qc, kc, vc, kx_l, vx_l)


def _tile_scan(s_ref, hb_ref, tab_ref, d, row0, carry, reverse):
    n16 = SSM_CHUNK // 16

    def body(i, carry):
        blk = (n16 - 1 - i) if reverse else i
        r = pl.multiple_of(row0 + blk * 16, 16)
        halves = [None, None]
        for half in ((1, 0) if reverse else (0, 1)):
            x = s_ref[pl.ds(pl.multiple_of(r + half * 8, 8), 8), :]
            xr, xi = x[:, 0:SSM_N], x[:, SSM_N:2 * SSM_N]
            for ki, k in enumerate((1, 2, 4)):
                ar = tab_ref[d, ki, :, 0:SSM_N]
                ai = tab_ref[d, ki, :, SSM_N:2 * SSM_N]
                shift = (8 - k) if reverse else k
                sr = pltpu.roll(xr, shift, 0)
                si = pltpu.roll(xi, shift, 0)
                xr, xi = xr + (ar * sr - ai * si), xi + (ar * si + ai * sr)
            pr = tab_ref[d, 3, :, 0:SSM_N]
            pi = tab_ref[d, 3, :, SSM_N:2 * SSM_N]
            cr = jnp.broadcast_to(carry[:, 0:SSM_N], (8, SSM_N))
            ci = jnp.broadcast_to(carry[:, SSM_N:2 * SSM_N], (8, SSM_N))
            xr, xi = xr + (pr * cr - pi * ci), xi + (pr * ci + pi * cr)
            last = 0 if reverse else 7
            carry = jnp.concatenate([xr[last:last + 1], xi[last:last + 1]], axis=1)
            halves[half] = jnp.concatenate([xr, xi], axis=1)
        hb_ref[pl.ds(r, 16), :] = jnp.concatenate(halves, axis=0).astype(BF16)
        return carry

    return lax.fori_loop(0, n16, body, carry)


def _ssm_kernel(uf_ref, ub_ref, bmat_ref, cmat_ref, tab_ref, h0_ref, yf_ref, yb_ref, fin_ref,
                s_ref, hb_ref, carry_ref):
    nseq = uf_ref.shape[0]
    rows = SSM_GROUP * SSM_CHUNK
    step = 2 * SSM_CHUNK

    @pl.when(pl.program_id(1) == 0)
    def _():
        carry_ref[...] = h0_ref[...]

    for d, u_ref, y_ref in ((0, uf_ref, yf_ref), (1, ub_ref, yb_ref)):
        for g0 in range(0, nseq, SSM_GROUP):
            u_bf = u_ref[g0:g0 + SSM_GROUP].reshape(rows, SSM_W).astype(BF16)
            for r in range(0, rows, step):
                s_ref[r:r + step, :] = _dot(u_bf[r:r + step], bmat_ref[d])
            for b in range(SSM_GROUP):
                carry = carry_ref[d, g0 + b:g0 + b + 1, :]
                carry = _tile_scan(s_ref, hb_ref, tab_ref, d, b * SSM_CHUNK, carry, d == 1)
                carry_ref[d, g0 + b:g0 + b + 1, :] = carry
            for r in range(0, rows, step):
                y = _dot(hb_ref[r:r + step, :], cmat_ref[d])
                for b in range(step // SSM_CHUNK):
                    y_ref[g0 + r // SSM_CHUNK + b] = y[b * SSM_CHUNK:(b + 1) * SSM_CHUNK]
    fin_ref[...] = carry_ref[...]


def _ssm(u_fwd_spec, u_bwd_spec, y_fwd_spec, y_bwd_spec, y_shape, grid, nseq, nseq_total, name):
    state = pl.BlockSpec((2, nseq, 2 * SSM_N), lambda g, c: (0, g, 0))
    rows = SSM_GROUP * SSM_CHUNK
    return pl.pallas_call(
        _ssm_kernel,
        grid=grid,
        in_specs=[u_fwd_spec, u_bwd_spec, _full((2, SSM_W, 2 * SSM_N)), _full((2, 2 * SSM_N, SSM_W)),
                  _full((2, 4, 8, 2 * SSM_N)), state],
        out_specs=[y_fwd_spec, y_bwd_spec, state],
        out_shape=[jax.ShapeDtypeStruct(y_shape, F32), jax.ShapeDtypeStruct(y_shape, F32),
                   jax.ShapeDtypeStruct((2, nseq_total, 2 * SSM_N), F32)],
        scratch_shapes=[pltpu.VMEM((rows, 2 * SSM_N), F32), pltpu.VMEM((rows, 2 * SSM_N), BF16),
                        pltpu.VMEM((2, nseq, 2 * SSM_N), F32)],
        compiler_params=_params(("parallel", "arbitrary")),
        name=name,
    )


def _ssm_ctx(u, bmat, cmat, tab, h0):
    nseq = 8
    blk = pl.BlockSpec((nseq, CTX_L, SSM_W), lambda g, c: (LAT_SEQ_BLOCKS // nseq + g, 0, 0))
    out = pl.BlockSpec((nseq, CTX_L, SSM_W), lambda g, c: (g, 0, 0))
    call = _ssm(blk, blk, out, out, (N_CTX_B, CTX_L, SSM_W), (N_CTX_B // nseq, 1), nseq, N_CTX_B, "ssm_ctx")
    u3 = u.reshape(T_ALL // CTX_L, CTX_L, SSM_W)
    return call(u3, u3, bmat, cmat, tab, h0)


def _ssm_lat(u, bmat, cmat, tab, h0):
    nc = LAT_L // SSM_CHUNK
    fwd = pl.BlockSpec((N_LAT_B, None, SSM_CHUNK, SSM_W), lambda g, c: (0, c, 0, 0))
    bwd = pl.BlockSpec((N_LAT_B, None, SSM_CHUNK, SSM_W), lambda g, c: (0, nc - 1 - c, 0, 0))
    call = _ssm(fwd, bwd, fwd, bwd, (N_LAT_B, nc, SSM_CHUNK, SSM_W), (1, nc), N_LAT_B, N_LAT_B, "ssm_lat")
    u4 = u.reshape(T_ALL // LAT_L, nc, SSM_CHUNK, SSM_W)
    return call(u4, u4, bmat, cmat, tab, h0)


def _gelu_tanh(x):
    return 0.5 * x * (1.0 + jnp.tanh(math.sqrt(2.0 / math.pi) * (x + 0.044715 * (x * x * x))))


def _outproj_kernel(x_ref, u_ref, oal_ref, oac_ref, ocl_ref, occ_ref, yfl_ref, yfc_ref, ybl_ref, ybc_ref,
                    mod_ref, d_ref, wglu_ref, woa_ref, wob_ref, woc_ref, gffn_ref, wrhi_ref, wrlo_ref, rb_ref,
                    x1_ref, h2_ref, comb_ref):
    is_lat = pl.program_id(0) < N_LAT_TILES
    oa = jnp.where(is_lat, oal_ref[...], oac_ref[...])
    oc = jnp.where(is_lat, ocl_ref[...], occ_ref[...])
    y = d_ref[...] * u_ref[...] + jnp.where(is_lat, yfl_ref[...] + ybl_ref[...], yfc_ref[...] + ybc_ref[...])
    g = _gelu_tanh(y)
    ob = g * jax.nn.sigmoid(_dot(g.astype(BF16), wglu_ref[...]))
    mix = _dot(oa, woa_ref[...]) + _dot(ob.astype(BF16), wob_ref[...]) + _dot(oc, woc_ref[...])
    x1 = x_ref[...] + mod_ref[2:3, :] * mix
    x1_ref[...] = x1
    ms = jnp.mean(x1 * x1, axis=-1, keepdims=True)
    h2 = x1 * lax.rsqrt(ms + EPS) * gffn_ref[...]
    h2 = h2 * (1.0 + mod_ref[4:5, :]) + mod_ref[3:4, :]
    h2_hi, h2_lo = _split_bf16(h2)
    h2_ref[...] = h2_hi

    lt = _dot_nt(wrhi_ref[...], h2_hi) + _dot_nt(wrhi_ref[...], h2_lo) + _dot_nt(wrlo_ref[...], h2_hi)
    sc = [jax.nn.sigmoid(lt[e:e + 1, :]) for e in range(N_EXP)]
    bi = [sc[e] + rb_ref[e:e + 1, 0:1] for e in range(N_EXP)]
    gs = []
    for gi in range(N_GRP):
        v = bi[gi * GRP_SZ:(gi + 1) * GRP_SZ]
        best2 = None
        for i in range(GRP_SZ):
            for j in range(i + 1, GRP_SZ):
                pair = v[i] + v[j]
                best2 = pair if best2 is None else jnp.maximum(best2, pair)
        gs.append(best2)
    best_g = jnp.zeros((1, TM), jnp.int32)
    best_v = gs[0]
    for gi in range(1, N_GRP):
        upd = gs[gi] > best_v
        best_g = jnp.where(upd, gi, best_g)
        best_v = jnp.where(upd, gs[gi], best_v)
    wts = []
    for e in range(N_EXP):
        gi = e // GRP_SZ
        beaten = jnp.zeros((1, TM), F32)
        for j in range(gi * GRP_SZ, (gi + 1) * GRP_SZ):
            if j == e:
                continue
            if j < e:
                beaten = beaten + jnp.where(bi[j] >= bi[e], 1.0, 0.0)
            else:
                beaten = beaten + jnp.where(bi[j] > bi[e], 1.0, 0.0)
        keep = jnp.where(best_g == gi, jnp.where(beaten < 1.5, 1.0, 0.0), 0.0)
        wts.append(keep * sc[e])
    tot = wts[0]
    for e in range(1, N_EXP):
        tot = tot + wts[e]
    inv = 1.0 / tot
    comb_t = jnp.concatenate([w_ * inv for w_ in wts] + [best_g.astype(F32)]
                             + [jnp.zeros((LANES - N_EXP - 1, TM), F32)], axis=0)
    comb_ref[...] = comb_t.T


def _outproj(x, u, oa_lat, oa_ctx, oc_lat, oc_ctx, yf_lat, yf_ctx, yb_lat, yb_ctx, mod_l, w, wr_hi, wr_lo, rb):
    nt = T_ALL // TM
    tile = lambda width: pl.BlockSpec((TM, width), lambda i: (i, 0))
    lat = lambda width: pl.BlockSpec((TM, width), lambda i: (jnp.minimum(i, N_LAT_TILES - 1), 0))
    ctx = lambda width: pl.BlockSpec((TM, width), lambda i: (jnp.maximum(i - N_LAT_TILES, 0), 0))
    in_specs = [
        tile(D), tile(256), lat(512), ctx(512), lat(256), ctx(256), lat(256), ctx(256), lat(256), ctx(256),
        pl.BlockSpec((None, 6, D), lambda i: (_mod_row(i), 0, 0)),
        _full((1, SSM_W)), _full((SSM_W, SSM_W)),
        _full((512, D)), _full((256, D)), _full((256, D)), _full((1, D)),
        _full((LANES, D)), _full((LANES, D)), _full((N_EXP, LANES)),
    ]
    return pl.pallas_call(
        _outproj_kernel,
        grid=(nt,),
        in_specs=in_specs,
        out_specs=[tile(D), tile(D), tile(LANES)],
        out_shape=[jax.ShapeDtypeStruct((T_ALL, D), F32), jax.ShapeDtypeStruct((T_ALL, D), BF16),
                   jax.ShapeDtypeStruct((T_ALL, LANES), F32)],
        compiler_params=_params(("parallel",)),
        name="outproj_router",
    )(x, u, oa_lat, oa_ctx, oc_lat, oc_ctx, yf_lat, yf_ctx, yb_lat, yb_ctx, mod_l, w["ssm_d"], w["wglu"],
      w["woa"], w["wob"], w["woc"], w["gffn"], wr_hi, wr_lo, rb)


def _moe_route(h_ref, comb_ref, xs_ref, ws_ref, ys_ref, pt_ref, seg_ref):
    lane = lax.broadcasted_iota(jnp.int32, (TM_MOE, LANES), 1)
    comb = comb_ref[...]
    grp = jnp.sum(jnp.where(lane == N_EXP, comb, 0.0), axis=-1, keepdims=True).astype(jnp.int32)
    onehot = jnp.where(lane == grp, 1.0, 0.0)
    tok_r = lax.broadcasted_iota(jnp.int32, (TM_MOE, TM_MOE), 0)
    tok_c = lax.broadcasted_iota(jnp.int32, (TM_MOE, TM_MOE), 1)
    earlier = jnp.where(tok_c < tok_r, 1.0, 0.0).astype(BF16)
    rank = _dot(earlier, onehot.astype(BF16))
    count = jnp.sum(onehot, axis=0, keepdims=True).astype(jnp.int32)
    padded = lax.shift_left(lax.shift_right_logical(count + (MOE_CHUNK - 1), MOE_SHIFT), MOE_SHIFT)
    lane1 = lax.broadcasted_iota(jnp.int32, (1, LANES), 1)
    base = jnp.zeros((1, LANES), jnp.int32)
    run = jnp.zeros((1, 1), jnp.int32)
    for g in range(N_GRP):
        base = jnp.where(lane1 == g, run, base)
        seg_ref[2 * g] = lax.shift_right_logical(run, MOE_SHIFT)[0, 0]
        seg_ref[2 * g + 1] = lax.shift_right_logical(padded[:, g:g + 1], MOE_SHIFT)[0, 0]
        run = run + padded[:, g:g + 1]
    slot = jnp.sum(onehot * (base.astype(F32) + rank), axis=-1, keepdims=True)
    slot_lane = lax.broadcasted_iota(jnp.int32, (TM_MOE, MOE_SLOTS), 1)
    pt_ref[...] = jnp.where(slot_lane == slot.astype(jnp.int32), 1.0, 0.0).astype(BF16)
    slot_row = jnp.broadcast_to(slot, (TM_MOE, LANES)).T[0:1, :].astype(jnp.int32)

    c1 = comb.astype(BF16).astype(F32)
    c2 = (comb - c1).astype(BF16).astype(F32)
    c3 = comb - c1 - c2
    pieces = (c1 + pltpu.roll(c2, 32, 1) + pltpu.roll(c3, 64, 1)).astype(BF16)
    x = h_ref[...]
    step = 512
    for r0 in range(0, MOE_SLOTS, step):
        slot_sub = lax.broadcasted_iota(jnp.int32, (step, TM_MOE), 0) + r0
        p = jnp.where(slot_sub == slot_row, 1.0, 0.0).astype(BF16)
        xs_ref[r0:r0 + step, :] = _dot(p, x).astype(BF16)
        wp = _dot(p, pieces)
        ws_ref[r0:r0 + step, :] = wp + pltpu.roll(wp, LANES - 32, 1) + pltpu.roll(wp, LANES - 64, 1)
    ys_ref[...] = jnp.zeros_like(ys_ref)


def _moe_kernel(h_ref, comb_ref, x1_ref, mod_ref, wg_ref, wu_ref, wd_ref, o_ref,
                xs_ref, ws_ref, ys_ref, pt_ref, seg_ref):
    e = pl.program_id(1)

    @pl.when(e == 0)
    def _():
        _moe_route(h_ref, comb_ref, xs_ref, ws_ref, ys_ref, pt_ref, seg_ref)

    g = e // GRP_SZ
    first = seg_ref[2 * g]
    count = seg_ref[2 * g + 1]

    def run(chunk, nrows):
        r = pl.multiple_of(chunk * MOE_CHUNK, MOE_CHUNK)
        xc = xs_ref[pl.ds(r, nrows), :]
        lane = lax.broadcasted_iota(jnp.int32, (nrows, LANES), 1)
        wcol = jnp.sum(jnp.where(lane == e, ws_ref[pl.ds(r, nrows), :], 0.0), axis=-1, keepdims=True)
        y = None
        for h0 in range(0, D_EXP, D_EXP // 2):
            hs = slice(h0, h0 + D_EXP // 2)
            a = _dot(xc, wg_ref[:, hs])
            b = _dot(xc, wu_ref[:, hs])
            act = ((a * jax.nn.sigmoid(a)) * b).astype(BF16)
            part = _dot(act, wd_ref[hs, :])
            y = part if y is None else y + part
        ys_ref[pl.ds(r, nrows), :] += wcol * y

    def triple(c, carry):
        run(first + 3 * c, 3 * MOE_CHUNK)
        return carry

    triples = count // 3
    lax.fori_loop(0, triples, triple, 0)
    left = count - 3 * triples

    @pl.when(left == 2)
    def _():
        run(first + count - 2, 2 * MOE_CHUNK)

    @pl.when(left == 1)
    def _():
        run(first + count - 1, MOE_CHUNK)

    @pl.when(e == N_EXP - 1)
    def _():
        ysb = ys_ref[...].astype(BF16)
        step = 256
        for r0 in range(0, TM_MOE, step):
            y = _dot(pt_ref[r0:r0 + step, :], ysb)
            o_ref[r0:r0 + step, :] = x1_ref[r0:r0 + step, :] + mod_ref[5:6, :] * y


def _moe(h2, comb, x1, mod_l, wg, wu, wd):
    nt = T_ALL // TM_MOE
    per = TM_MOE // TM
    tile = lambda width: pl.BlockSpec((TM_MOE, width), lambda i, e: (i, 0))
    return pl.pallas_call(
        _moe_kernel,
        grid=(nt, N_EXP),
        in_specs=[
            tile(D), tile(LANES), tile(D),
            pl.BlockSpec((None, 6, D), lambda i, e: (_mod_row(i * per), 0, 0)),
            pl.BlockSpec((None, D, D_EXP), lambda i, e: (e, 0, 0)),
            pl.BlockSpec((None, D, D_EXP), lambda i, e: (e, 0, 0)),
            pl.BlockSpec((None, D_EXP, D), lambda i, e: (e, 0, 0)),
        ],
        out_specs=tile(D),
        out_shape=jax.ShapeDtypeStruct((T_ALL, D), F32),
        scratch_shapes=[pltpu.VMEM((MOE_SLOTS, D), BF16), pltpu.VMEM((MOE_SLOTS, LANES), F32),
                        pltpu.VMEM((MOE_SLOTS, D), F32), pltpu.VMEM((TM_MOE, MOE_SLOTS), BF16),
                        pltpu.SMEM((2 * N_GRP,), jnp.int32)],
        compiler_params=pltpu.CompilerParams(dimension_semantics=("parallel", "arbitrary"),
                                             vmem_limit_bytes=MOE_VMEM_LIMIT),
        name="moe",
    )(h2, comb, x1, mod_l, wg, wu, wd)


def _pad_heads(w, heads, dim):
    k = w.shape[0]
    return jnp.pad(w.reshape(k, heads, dim), ((0, 0), (0, 0), (0, LANES - dim))).reshape(k, heads * LANES)


def _block_diag_b(bb_t):
    eye = jnp.eye(SSM_G, dtype=bb_t.dtype)
    return jnp.einsum("ghp,gk->ghkp", bb_t, eye).reshape(SSM_W, SSM_N)


def _block_diag_c(c):
    eye = jnp.eye(SSM_G, dtype=c.dtype)
    return jnp.einsum("ghp,gk->gpkh", c, eye).reshape(SSM_N, SSM_W)


def _scan_tables(pw_re, pw_im):
    def flat(p):
        return jnp.transpose(p, (0, 2, 1, 3)).reshape(2, 8, SSM_N)

    pw = jnp.concatenate([flat(pw_re), flat(pw_im)], axis=-1)
    pos = jnp.arange(8)[:, None]
    tabs = []
    for d in range(2):
        rows = []
        for k in (1, 2, 4):
            keep = (pos >= k) if d == 0 else (pos <= 7 - k)
            rows.append(jnp.where(keep, pw[d, k - 1][None, :], 0.0))
        rows.append(pw[d] if d == 0 else pw[d, ::-1])
        tabs.append(jnp.stack(rows))
    return jnp.stack(tabs)


def kernel(x_prompt, x_sample, cache_attn_k, cache_attn_v, cache_mla_ckv, cache_mla_kpe, state_ssm_re,
           state_ssm_im, c, c_ctx, norm_mix, norm_ffn, w_ada, b_ada, w_in, a_q_norm, a_k_norm, a_sink,
           ssm_a_re, ssm_a_im, ssm_log_dt, ssm_b_re, ssm_b_im, ssm_c_re, ssm_c_im, ssm_d, w_glu,
           mla_q_norm, mla_kv_norm, w_q_b, w_kv_b, mla_qh_norm, mla_kh_norm, w_out, w_router,
           router_bias, w_gate, w_up, w_down):
    x = jnp.concatenate([x_sample.reshape(T_LAT, D), x_prompt.reshape(T_CTX, D)], axis=0)

    cond = jnp.concatenate([c_ctx[None, :], c, jnp.zeros((16 - 1 - N_LAT_B, D), F32)], axis=0)
    mod = _adaln(cond, w_ada, b_ada)

    pw_re, pw_im, bbt_re, bbt_im = _discretize(ssm_a_re, ssm_a_im, ssm_log_dt, ssm_b_re, ssm_b_im)
    pw_re = pw_re.reshape(DEPTH, 2, SSM_G, 8, SSM_P)
    pw_im = pw_im.reshape(DEPTH, 2, SSM_G, 8, SSM_P)
    bbt_re = bbt_re.reshape(DEPTH, 2, SSM_G, SSM_H, SSM_P)
    bbt_im = bbt_im.reshape(DEPTH, 2, SSM_G, SSM_H, SSM_P)
    h0_ctx = jnp.zeros((2, N_CTX_B, 2 * SSM_N), F32)

    tab_a, tab_c = _rope_tables()

    wr_t = jnp.pad(w_router.T, ((0, LANES - N_EXP), (0, 0)))
    wr_hi = wr_t.astype(BF16)
    wr_lo = (wr_t - wr_hi.astype(F32)).astype(BF16)
    rb = jnp.broadcast_to(router_bias[:, None], (N_EXP, LANES))

    w_kv4 = w_kv_b.reshape(DEPTH, C_KVLORA, C_HEADS, C_NOPE + C_V)
    wkn_all = jnp.pad(w_kv4[..., :C_NOPE], ((0, 0), (0, 0), (0, 0), (0, LANES - C_NOPE))
                      ).reshape(DEPTH, C_KVLORA, C_HEADS * LANES).astype(BF16)
    wv_all = w_kv4[..., C_NOPE:].reshape(DEPTH, C_KVLORA, C_HEADS * C_V).astype(BF16)
    gkh_all = jnp.pad(mla_kh_norm, ((0, 0), (0, LANES - C_QK))).reshape(DEPTH, 1, LANES)

    kpe_slab = jnp.pad(cache_mla_kpe, ((0, 0), (0, 0), (0, 0), (C_NOPE, LANES - C_QK)))
    kx_all, vx_all = _cache_kv(cache_mla_ckv, kpe_slab, wkn_all, wv_all, gkh_all)

    cache_k = cache_attn_k.reshape(N_LAT_B, DEPTH, CTX_L, A_KV * A_DH)
    cache_v = cache_attn_v.reshape(N_LAT_B, DEPTH, CTX_L, A_KV * A_DH)

    ks, vs, ckvs, kpes, fins = [], [], [], [], []
    for l in range(DEPTH):
        wl = w_in[l]
        w = {
            "gmix": norm_mix[l][None, :],
            "wq": wl[:, 0:512].astype(BF16),
            "wkv": wl[:, 512:768].astype(BF16),
            "wu": wl[:, 768:1024].astype(BF16),
            "wcq": wl[:, 1024:1280].astype(BF16),
            "wt": jnp.pad(wl[:, 1280:1440], ((0, 0), (0, 256 - 160))).astype(BF16),
            "gq": jnp.tile(a_q_norm[l], 2)[None, :],
            "gk": jnp.tile(a_k_norm[l], 2)[None, :],
            "gcq": mla_q_norm[l][None, :],
            "gckv": mla_kv_norm[l][None, :],
            "wqb": _pad_heads(w_q_b[l], C_HEADS, C_QK).astype(BF16),
            "wkn": wkn_all[l],
            "wv": wv_all[l],
            "gqh": jnp.pad(mla_qh_norm[l], (0, LANES - C_QK))[None, :],
            "gkh": gkh_all[l],
            "ssm_d": ssm_d[l].reshape(1, SSM_W),
            "wglu": w_glu[l].astype(BF16),
            "woa": w_out[l, 0:512].astype(BF16),
            "wob": w_out[l, 512:768].astype(BF16),
            "woc": w_out[l, 768:1024].astype(BF16),
            "gffn": norm_ffn[l][None, :],
        }
        mod_l = mod[l]
        qa, ka, va, vaug, u, qc, kc, vc, ckv_n, kpe = _proj(x, mod_l, w, tab_a, tab_c)

        sink = jnp.broadcast_to(a_sink[l][:, None], (A_HEADS, LANES))
        oa_ctx = _attn_a_ctx(qa, ka, vaug, sink)
        oa_lat = _attn_a_lat(qa, ka, vaug, cache_k[:, l], cache_v[:, l], sink)
        oc_ctx = _mla_ctx(qc, kc, vc)
        oc_lat = _mla_lat(qc, kc, vc, kx_all[l], vx_all[l])

        bmat = jnp.stack([jnp.concatenate([_block_diag_b(bbt_re[l, d]), _block_diag_b(bbt_im[l, d])], axis=1)
                          for d in range(2)]).astype(BF16)
        cmat = jnp.stack([jnp.concatenate([_block_diag_c(ssm_c_re[l, d]), -_block_diag_c(ssm_c_im[l, d])], axis=0)
                          for d in range(2)]).astype(BF16)
        tab = _scan_tables(pw_re[l], pw_im[l])
        yf_ctx, yb_ctx, fin = _ssm_ctx(u, bmat, cmat, tab, h0_ctx)
        h0 = jnp.swapaxes(jnp.concatenate([state_ssm_re[:, l].reshape(N_LAT_B, 2, SSM_N),
                                            state_ssm_im[:, l].reshape(N_LAT_B, 2, SSM_N)], axis=-1), 0, 1)
        yf_lat, yb_lat, _ = _ssm_lat(u, bmat, cmat, tab, h0)

        x1, h2, comb = _outproj(x, u, oa_lat, oa_ctx, oc_lat, oc_ctx,
                                yf_lat.reshape(T_LAT, SSM_W), yf_ctx.reshape(T_CTX, SSM_W),
                                yb_lat.reshape(T_LAT, SSM_W), yb_ctx.reshape(T_CTX, SSM_W),
                                mod_l, w, wr_hi, wr_lo, rb)
        x = _moe(h2, comb, x1, mod_l, w_gate[l].astype(BF16), w_up[l].astype(BF16), w_down[l].astype(BF16))

        ks.append(ka[T_LAT:].reshape(N_CTX_B, CTX_L, A_KV, A_DH))
        vs.append(va[T_LAT:].reshape(N_CTX_B, CTX_L, A_KV, A_DH))
        ckvs.append(ckv_n[T_LAT:].reshape(N_CTX_B, CTX_L, C_KVLORA))
        kpes.append(kpe[T_LAT:].reshape(N_CTX_B, CTX_L, C_ROPE))
        fins.append(fin)

    y_prompt = x[T_LAT:].reshape(N_CTX_B, CTX_L, D)
    y_sample = x[:T_LAT].reshape(N_LAT_B, LAT_L, D)
    fin = jnp.transpose(jnp.stack(fins, axis=0), (2, 0, 1, 3))
    new_re = fin[..., 0:SSM_N].reshape(N_CTX_B, DEPTH, 2, SSM_G, SSM_P)
    new_im = fin[..., SSM_N:].reshape(N_CTX_B, DEPTH, 2, SSM_G, SSM_P)
    return (y_prompt, y_sample, jnp.stack(ks, axis=1), jnp.stack(vs, axis=1), jnp.stack(ckvs, axis=1),
            jnp.stack(kpes, axis=1), new_re, new_im)
```

```python
import functools
import math

import jax
import jax.numpy as jnp
from jax import lax
from jax.experimental import pallas as pl
from jax.experimental.pallas import tpu as pltpu

F32 = jnp.float32
BF16 = jnp.bfloat16

D = 1024
N_CTX_B, CTX_L = 32, 256
N_LAT_B, LAT_L = 4, 4096
DEPTH = 4
GRID_W = 64
EPS = 1e-6
ROPE_BASE = 10000.0
A_HEADS, A_KV, A_DH = 8, 2, 64
SSM_G, SSM_H, SSM_P = 16, 16, 64
SSM_W = SSM_G * SSM_H
SSM_N = SSM_G * SSM_P
SSM_SLABS = SSM_N // 128
C_HEADS, C_NOPE, C_ROPE, C_V = 4, 64, 32, 64
C_QK = C_NOPE + C_ROPE
C_QLORA, C_KVLORA = 256, 128
N_EXP, N_GRP, GRP_SZ, D_EXP = 16, 4, 4, 512
WINDOW = 128

T_CTX = N_CTX_B * CTX_L
T_LAT = N_LAT_B * LAT_L
T_ALL = T_CTX + T_LAT

LANES = 128
HALF = LANES // 2
VMEM_LIMIT = 48 * 1024 * 1024

TM = 512
N_LAT_TILES = T_LAT // TM
LAT_TILES_PER_B = LAT_L // TM
LAT_SEQ_BLOCKS = T_LAT // CTX_L
TQ_A = 256
WIN_KEYS = TQ_A + 2 * WINDOW + CTX_L
TQ_C = 256
CTX_ATTN_SEQS = 4
TM_MOE = 1024
MOE_SHIFT = 7
MOE_CHUNK = 1 << MOE_SHIFT
MOE_SLOTS = TM_MOE + N_GRP * MOE_CHUNK
MOE_VMEM_LIMIT = 56 * 1024 * 1024
SSM_CHUNK = 256
SSM_GROUP = 4
NEG = -1e30
LOG2E = math.log2(math.e)


def _dot(a, b):
    return jnp.dot(a, b, preferred_element_type=F32)


def _dot_nt(a, b):
    return lax.dot_general(a, b, (((1,), (1,)), ((), ())), preferred_element_type=F32)


def _split_bf16(x):
    hi = x.astype(BF16)
    lo = (x - hi.astype(F32)).astype(BF16)
    return hi, lo


def _params(sem):
    return pltpu.CompilerParams(dimension_semantics=sem, vmem_limit_bytes=VMEM_LIMIT)


def _full(shape):
    n = len(shape)
    return pl.BlockSpec(shape, lambda *_: (0,) * n)


def _mod_row(i):
    return jnp.where(i < N_LAT_TILES, 1 + i // LAT_TILES_PER_B, 0)


def _pos_block(i):
    return jnp.where(i < N_LAT_TILES, i % LAT_TILES_PER_B, LAT_TILES_PER_B)


def _adaln_kernel(c_ref, w_ref, b_ref, o_ref):
    c = c_ref[...]
    s = c * jax.nn.sigmoid(c)
    s_hi, s_lo = _split_bf16(s)
    w_hi, w_lo = _split_bf16(w_ref[...])
    o_ref[...] = _dot(s_hi, w_hi) + _dot(s_hi, w_lo) + _dot(s_lo, w_hi) + b_ref[...]


def _adaln(cond, w_ada, b_ada):
    nt = 6
    out = pl.pallas_call(
        _adaln_kernel,
        grid=(DEPTH, nt),
        in_specs=[
            pl.BlockSpec((16, D), lambda l, n: (0, 0)),
            pl.BlockSpec((None, D, D), lambda l, n: (l, 0, n)),
            pl.BlockSpec((None, 1, D), lambda l, n: (l, 0, n)),
        ],
        out_specs=pl.BlockSpec((None, 16, D), lambda l, n: (l, 0, n)),
        out_shape=jax.ShapeDtypeStruct((DEPTH, 16, nt * D), F32),
        compiler_params=_params(("parallel", "parallel")),
        name="adaln",
    )(cond, w_ada, b_ada.reshape(DEPTH, 1, nt * D))
    return out.reshape(DEPTH, 16, nt, D)


def _disc_kernel(are_ref, aim_ref, ldt_ref, bre_ref, bim_ref, pwre_ref, pwim_ref, bbre_ref, bbim_ref):
    a_re = are_ref[...]
    a_im = aim_ref[...]
    dt = jnp.exp(ldt_ref[...])
    mag = jnp.exp(a_re * dt)
    ab_re = mag * jnp.cos(a_im * dt)
    ab_im = mag * jnp.sin(a_im * dt)
    den = a_re * a_re + a_im * a_im
    n_re = ab_re - 1.0
    k_re = (n_re * a_re + ab_im * a_im) / den
    k_im = (ab_im * a_re - n_re * a_im) / den
    b_re = bre_ref[...]
    b_im = bim_ref[...]
    bbre_ref[...] = k_re * b_re - k_im * b_im
    bbim_ref[...] = k_re * b_im + k_im * b_re
    p_re, p_im = ab_re, ab_im
    for k in range(8):
        pwre_ref[:, k:k + 1, :] = p_re
        pwim_ref[:, k:k + 1, :] = p_im
        p_re, p_im = p_re * ab_re - p_im * ab_im, p_re * ab_im + p_im * ab_re


def _discretize(ssm_a_re, ssm_a_im, ssm_log_dt, ssm_b_re, ssm_b_im):
    n = DEPTH * 2 * SSM_G
    a_re = ssm_a_re.reshape(n, 1, SSM_P)
    a_im = ssm_a_im.reshape(n, 1, SSM_P)
    ldt = jnp.broadcast_to(ssm_log_dt.reshape(n, 1, 1), (n, 1, SSM_P))
    b_re = jnp.swapaxes(ssm_b_re.reshape(n, SSM_P, SSM_H), 1, 2)
    b_im = jnp.swapaxes(ssm_b_im.reshape(n, SSM_P, SSM_H), 1, 2)
    sa = jax.ShapeDtypeStruct((n, 8, SSM_P), F32)
    sb = jax.ShapeDtypeStruct((n, SSM_H, SSM_P), F32)
    return pl.pallas_call(
        _disc_kernel,
        out_shape=(sa, sa, sb, sb),
        compiler_params=pltpu.CompilerParams(vmem_limit_bytes=VMEM_LIMIT),
        name="s5_discretize",
    )(a_re, a_im, ldt, b_re, b_im)


def _rope_tables():
    t = jnp.arange(LAT_L)
    row = (t // GRID_W).astype(F32)[:, None]
    col = (t % GRID_W).astype(F32)[:, None]
    lane = jnp.arange(LANES)

    def build(d, width, valid):
        half = width // 2
        q = half // 2
        on_col = d >= half
        dd = jnp.where(on_col, d - half, d)
        first = dd < q
        j = jnp.where(first, dd, dd - q).astype(F32)
        inv = ROPE_BASE ** (-j / q)
        ang = jnp.where(on_col[None, :], col, row) * inv[None, :]
        cos = jnp.where(valid[None, :], jnp.cos(ang), 1.0)
        sin = jnp.where(valid[None, :], jnp.sin(ang), 0.0)
        s_up = jnp.where(first[None, :], -sin, 0.0)
        s_dn = jnp.where(first[None, :], 0.0, sin)
        tab = jnp.concatenate([cos, s_up, s_dn], axis=1)
        ident = jnp.concatenate([jnp.ones((TM, LANES), F32), jnp.zeros((TM, 2 * LANES), F32)], axis=1)
        return jnp.concatenate([tab, ident], axis=0)

    tab_a = build(lane % A_DH, A_DH, jnp.ones((LANES,), bool))
    dc = jnp.clip(lane - C_NOPE, 0, C_ROPE - 1)
    tab_c = build(dc, C_ROPE, (lane >= C_NOPE) & (lane < C_QK))
    return tab_a, tab_c


def _rope(x, tab_ref, shift):
    cos = tab_ref[:, 0:LANES]
    s_up = tab_ref[:, LANES:2 * LANES]
    s_dn = tab_ref[:, 2 * LANES:3 * LANES]
    return x * cos + pltpu.roll(x, LANES - shift, 1) * s_up + pltpu.roll(x, shift, 1) * s_dn


def _half_norm(xs, gain):
    lo = lax.broadcasted_iota(jnp.int32, xs.shape, 1) < A_DH
    sq = xs * xs
    s_lo = jnp.sum(jnp.where(lo, sq, 0.0), axis=-1, keepdims=True)
    s_hi = jnp.sum(jnp.where(lo, 0.0, sq), axis=-1, keepdims=True)
    r = jnp.where(lo, lax.rsqrt(s_lo * (1.0 / A_DH) + EPS), lax.rsqrt(s_hi * (1.0 / A_DH) + EPS))
    return xs * r * gain


def _with_ones(v, heads):
    lo = lax.broadcasted_iota(jnp.int32, (v.shape[0], LANES), 1) < HALF
    slabs = []
    for h in range(heads):
        pair = v[:, (h // 2) * LANES:(h // 2 + 1) * LANES]
        own = pair if h % 2 == 0 else pltpu.roll(pair, HALF, 1)
        slabs.append(jnp.where(lo, own, 1.0).astype(BF16))
    return slabs


def _normalise(o):
    return o / pltpu.roll(o, HALF, 1)


def _store_head_pairs(rs, o_ref):
    lo = lax.broadcasted_iota(jnp.int32, rs[0].shape, 1) < HALF
    for j in range(len(rs) // 2):
        slab = jnp.where(lo, rs[2 * j], pltpu.roll(rs[2 * j + 1], HALF, 1))
        o_ref[:, j * LANES:(j + 1) * LANES] = slab.astype(o_ref.dtype)


def _mla_keys_values(ckv_n, kpe_slab, wkn_ref, wv_ref, gkh, tab_ref):
    cb = ckv_n.astype(BF16)
    kn = _dot(cb, wkn_ref[...])
    v = _dot(cb, wv_ref[...])
    ks = []
    for h in range(C_HEADS):
        slab = kn[:, h * LANES:(h + 1) * LANES] + kpe_slab
        ss = jnp.sum(slab * slab, axis=-1, keepdims=True)
        slab = slab * lax.rsqrt(ss * (1.0 / C_QK) + EPS) * gkh
        if tab_ref is not None:
            slab = _rope(slab, tab_ref, C_ROPE // 4)
        ks.append(slab.astype(BF16))
    return ks, v


def _proj_kernel(x_ref, mod_ref, gmix_ref, wq_ref, wkv_ref, wu_ref, wcq_ref, wt_ref,
                 gq_ref, gk_ref, gcq_ref, gckv_ref, wqb_ref, wkn_ref, wv_ref, gqh_ref, gkh_ref,
                 ta_ref, tc_ref,
                 qa_ref, kab_ref, vaug_ref, u_ref, qc_ref, kc_ref, vc_ref, ka_ref, va_ref, ckv_ref, kpe_ref):
    x = x_ref[...]
    sh = mod_ref[0:1, :]
    sc = mod_ref[1:2, :]
    ms = jnp.mean(x * x, axis=-1, keepdims=True)
    h = x * lax.rsqrt(ms + EPS) * gmix_ref[...]
    h = (h * (1.0 + sc) + sh).astype(BF16)

    lo = lax.broadcasted_iota(jnp.int32, (TM, LANES), 1) < A_DH

    p_q = _dot(h, wq_ref[...])
    for j in range(A_HEADS // 2):
        slab = _half_norm(p_q[:, j * LANES:(j + 1) * LANES], gq_ref[...])
        slab = _rope(slab, ta_ref, A_DH // 4) * (A_DH ** -0.5 * LOG2E)
        swapped = pltpu.roll(slab, A_DH, 1)
        kv_head = (2 * j) // (A_HEADS // A_KV)
        if kv_head == 0:
            q0 = jnp.where(lo, slab, 0.0)
            q1 = jnp.where(lo, swapped, 0.0)
        else:
            q0 = jnp.where(lo, 0.0, swapped)
            q1 = jnp.where(lo, 0.0, slab)
        qa_ref[:, (2 * j) * LANES:(2 * j + 1) * LANES] = q0.astype(BF16)
        qa_ref[:, (2 * j + 1) * LANES:(2 * j + 2) * LANES] = q1.astype(BF16)

    p_kv = _dot(h, wkv_ref[...])
    k = _rope(_half_norm(p_kv[:, 0:LANES], gk_ref[...]), ta_ref, A_DH // 4)
    ka_ref[...] = k
    kab_ref[...] = k.astype(BF16)
    va_ref[...] = p_kv[:, LANES:2 * LANES]
    for kvh, slab in enumerate(_with_ones(p_kv[:, LANES:2 * LANES], A_KV)):
        vaug_ref[:, kvh * LANES:(kvh + 1) * LANES] = slab

    u_ref[...] = _dot(h, wu_ref[...])

    p_cq = _dot(h, wcq_ref[...])
    ms = jnp.mean(p_cq * p_cq, axis=-1, keepdims=True)
    cq = (p_cq * lax.rsqrt(ms + EPS) * gcq_ref[...]).astype(BF16)
    q = _dot(cq, wqb_ref[...])
    for hd in range(C_HEADS):
        slab = q[:, hd * LANES:(hd + 1) * LANES]
        ss = jnp.sum(slab * slab, axis=-1, keepdims=True)
        slab = slab * lax.rsqrt(ss * (1.0 / C_QK) + EPS) * gqh_ref[...]
        slab = _rope(slab, tc_ref, C_ROPE // 4) * (C_QK ** -0.5 * LOG2E)
        qc_ref[:, hd * LANES:(hd + 1) * LANES] = slab.astype(BF16)

    p_t = _dot(h, wt_ref[...])
    ckv = p_t[:, 0:LANES]
    ms = jnp.mean(ckv * ckv, axis=-1, keepdims=True)
    ckv_n = ckv * lax.rsqrt(ms + EPS) * gckv_ref[...]
    ckv_ref[...] = ckv_n
    kpe_wide = p_t[:, LANES:2 * LANES]
    kpe_ref[...] = kpe_wide[:, 0:C_ROPE]
    kpe_slab = pltpu.roll(kpe_wide, C_NOPE, 1)
    ks, v = _mla_keys_values(ckv_n, kpe_slab, wkn_ref, wv_ref, gkh_ref[...], tc_ref)
    for hd, slab in enumerate(_with_ones(v, C_HEADS)):
        kc_ref[:, hd * LANES:(hd + 1) * LANES] = ks[hd]
        vc_ref[:, hd * LANES:(hd + 1) * LANES] = slab


def _proj(x, mod_l, w, tab_a, tab_c):
    nt = T_ALL // TM
    tile = lambda width: pl.BlockSpec((TM, width), lambda i: (i, 0))
    tab = pl.BlockSpec((TM, 3 * LANES), lambda i: (_pos_block(i), 0))
    in_specs = [
        tile(D),
        pl.BlockSpec((None, 6, D), lambda i: (_mod_row(i), 0, 0)),
        _full((1, D)),
        _full((D, 512)), _full((D, 256)), _full((D, 256)), _full((D, 256)), _full((D, 256)),
        _full((1, LANES)), _full((1, LANES)), _full((1, 256)), _full((1, LANES)),
        _full((256, 512)), _full((LANES, 512)), _full((LANES, 256)),
        _full((1, LANES)), _full((1, LANES)),
        tab, tab,
    ]
    ctx = lambda width: pl.BlockSpec((TM, width), lambda i: (jnp.maximum(i - N_LAT_TILES, 0), 0))
    outs = [(1024, BF16), (128, BF16), (256, BF16), (256, F32), (512, BF16), (512, BF16), (512, BF16)]
    ctx_outs = [128, 128, 128, C_ROPE]
    return pl.pallas_call(
        _proj_kernel,
        grid=(nt,),
        in_specs=in_specs,
        out_specs=[tile(wd) for wd, _ in outs] + [ctx(wd) for wd in ctx_outs],
        out_shape=([jax.ShapeDtypeStruct((T_ALL, wd), dt) for wd, dt in outs]
                   + [jax.ShapeDtypeStruct((T_CTX, wd), F32) for wd in ctx_outs]),
        compiler_params=_params(("arbitrary",)),
        name="proj",
    )(x, mod_l, w["gmix"], w["wq"], w["wkv"], w["wu"], w["wcq"], w["wt"],
      w["gq"], w["gk"], w["gcq"], w["gckv"], w["wqb"], w["wkn"], w["wv"], w["gqh"], w["gkh"],
      tab_a, tab_c)


def _cache_kv_kernel(ckv_ref, kpe_ref, wkn_ref, wv_ref, gkh_ref, k_ref, v_ref):
    ks, v = _mla_keys_values(ckv_ref[...], kpe_ref[...], wkn_ref, wv_ref, gkh_ref[...], None)
    for hd, slab in enumerate(_with_ones(v, C_HEADS)):
        k_ref[:, hd * LANES:(hd + 1) * LANES] = ks[hd]
        v_ref[:, hd * LANES:(hd + 1) * LANES] = slab


def _cache_kv(cache_ckv, cache_kpe_slab, wkn, wv, gkh):
    return pl.pallas_call(
        _cache_kv_kernel,
        grid=(DEPTH, N_LAT_B),
        in_specs=[
            pl.BlockSpec((None, None, CTX_L, LANES), lambda l, b: (b, l, 0, 0)),
            pl.BlockSpec((None, None, CTX_L, LANES), lambda l, b: (b, l, 0, 0)),
            pl.BlockSpec((None, LANES, 512), lambda l, b: (l, 0, 0)),
            pl.BlockSpec((None, LANES, 256), lambda l, b: (l, 0, 0)),
            pl.BlockSpec((None, 1, LANES), lambda l, b: (l, 0, 0)),
        ],
        out_specs=[
            pl.BlockSpec((None, None, CTX_L, 512), lambda l, b: (l, b, 0, 0)),
            pl.BlockSpec((None, None, CTX_L, 512), lambda l, b: (l, b, 0, 0)),
        ],
        out_shape=[jax.ShapeDtypeStruct((DEPTH, N_LAT_B, CTX_L, 512), BF16),
                   jax.ShapeDtypeStruct((DEPTH, N_LAT_B, CTX_L, 512), BF16)],
        compiler_params=_params(("parallel", "parallel")),
        name="mla_cache_kv",
    )(cache_ckv, cache_kpe_slab, wkn, wv, gkh)


def _sink_column(sink_ref, kv_head, rows):
    grp = A_HEADS // A_KV
    return jnp.concatenate(
        [jnp.broadcast_to(sink_ref[kv_head * grp + g:kv_head * grp + g + 1, 0:1] * LOG2E, (rows, 1))
         for g in range(grp)],
        axis=0)


def _sink_softmax_head(q_ref, kv_head, rows, kb, vslab, bias, sink_ref):
    grp = A_HEADS // A_KV
    q = jnp.concatenate([q_ref[:, (kv_head * grp + g) * LANES:(kv_head * grp + g + 1) * LANES] for g in range(grp)],
                        axis=0)
    s = _dot_nt(q, kb)
    if bias is not None:
        s = s + jnp.concatenate([bias] * grp, axis=0)
    sk = _sink_column(sink_ref, kv_head, rows)
    m = jnp.maximum(jnp.max(s, axis=-1, keepdims=True), sk)
    o = _dot(jnp.exp2(s - m).astype(BF16), vslab)
    lo = lax.broadcasted_iota(jnp.int32, o.shape, 1) < HALF
    o = o + jnp.where(lo, 0.0, jnp.exp2(sk - m))
    r = _normalise(o)
    return [r[g * rows:(g + 1) * rows] for g in range(grp)]


def _attn_a_ctx_kernel(q_ref, k_ref, v_ref, sink_ref, o_ref):
    for i in range(CTX_ATTN_SEQS):
        kb = k_ref[i]
        rs = []
        for kvh in range(A_KV):
            rs.extend(_sink_softmax_head(q_ref.at[i], kvh, CTX_L, kb, v_ref[i, :, kvh * LANES:(kvh + 1) * LANES],
                                         None, sink_ref))
        _store_head_pairs(rs, o_ref.at[i])


def _ctx_seq_spec(width, nseq=CTX_ATTN_SEQS):
    return pl.BlockSpec((nseq, CTX_L, width), lambda g: (LAT_SEQ_BLOCKS // nseq + g, 0, 0))


def _by_seq(a):
    return a.reshape(T_ALL // CTX_L, CTX_L, a.shape[-1])


def _attn_a_ctx(qa, ka, va, sink):
    out = pl.pallas_call(
        _attn_a_ctx_kernel,
        grid=(N_CTX_B // CTX_ATTN_SEQS,),
        in_specs=[_ctx_seq_spec(1024), _ctx_seq_spec(LANES), _ctx_seq_spec(A_KV * LANES), _full((A_HEADS, LANES))],
        out_specs=pl.BlockSpec((CTX_ATTN_SEQS, CTX_L, 512), lambda g: (g, 0, 0)),
        out_shape=jax.ShapeDtypeStruct((N_CTX_B, CTX_L, 512), BF16),
        compiler_params=_params(("parallel",)),
        name="attn_a_ctx",
    )(_by_seq(qa), _by_seq(ka), _by_seq(va), sink)
    return out.reshape(T_CTX, 512)


def _attn_a_lat_kernel(q_ref, kp_ref, kc_ref, kn_ref, vp_ref, vc_ref, vn_ref, kx_ref, vx_ref, sink_ref, band_ref,
                       o_ref):
    grp = A_HEADS // A_KV
    n = pl.program_id(1)
    nb = LAT_L // TQ_A
    kb = jnp.concatenate([kp_ref[...], kc_ref[...], kn_ref[...], kx_ref[...].astype(BF16)], axis=0)
    vx = _with_ones(vx_ref[...], A_KV)
    kj = lax.broadcasted_iota(jnp.int32, (1, WIN_KEYS), 1)
    outside = ((kj < WINDOW) & (n == 0)) | ((kj >= WINDOW + TQ_A) & (kj < 2 * WINDOW + TQ_A) & (n == nb - 1))
    bias = band_ref[...] + jnp.where(outside, NEG, 0.0)
    rs = []
    for kvh in range(A_KV):
        hs = slice(kvh * LANES, (kvh + 1) * LANES)
        vslab = jnp.concatenate([vp_ref[:, hs], vc_ref[:, hs], vn_ref[:, hs], vx[kvh]], axis=0)
        rs.extend(_sink_softmax_head(q_ref, kvh, TQ_A, kb, vslab, bias, sink_ref))
    _store_head_pairs(rs, o_ref)


def _attn_a_lat(qa, ka, va, cache_k_l, cache_v_l, sink):
    nb = LAT_L // TQ_A
    base = 0

    per = TQ_A // WINDOW
    nw = LAT_L // WINDOW
    own = lambda width: pl.BlockSpec((TQ_A, width), lambda b, n: (base + b * nb + n, 0))
    prev = lambda width: pl.BlockSpec((WINDOW, width), lambda b, n: (b * nw + jnp.maximum(n * per - 1, 0), 0))
    nxt = lambda width: pl.BlockSpec((WINDOW, width), lambda b, n: (b * nw + jnp.minimum((n + 1) * per, nw - 1), 0))
    cache = pl.BlockSpec((None, CTX_L, LANES), lambda b, n: (b, 0, 0))
    vw = A_KV * LANES
    return pl.pallas_call(
        _attn_a_lat_kernel,
        grid=(N_LAT_B, nb),
        in_specs=[
            own(1024),
            prev(LANES), own(LANES), nxt(LANES), prev(vw), own(vw), nxt(vw),
            cache, cache, _full((A_HEADS, LANES)), _full((TQ_A, WIN_KEYS)),
        ],
        out_specs=pl.BlockSpec((TQ_A, 512), lambda b, n: (b * nb + n, 0)),
        out_shape=jax.ShapeDtypeStruct((T_LAT, 512), BF16),
        compiler_params=_params(("parallel", "parallel")),
        name="attn_a_lat",
    )(qa, ka, ka, ka, va, va, va, cache_k_l, cache_v_l, sink, _band_mask())


def _band_mask():
    qi = jnp.arange(TQ_A)[:, None]
    kj = jnp.arange(WIN_KEYS)[None, :]
    ok = (jnp.abs(kj - WINDOW - qi) <= WINDOW) | (kj >= TQ_A + 2 * WINDOW)
    return jnp.where(ok, 0.0, NEG).astype(F32)


def _mla_ctx_kernel(q_ref, k_ref, v_ref, o_ref):
    for i in range(q_ref.shape[0]):
        rs = []
        for h in range(C_HEADS):
            hs = slice(h * LANES, (h + 1) * LANES)
            s = _dot_nt(q_ref[i, :, hs], k_ref[i, :, hs])
            m = jnp.max(s, axis=-1, keepdims=True)
            rs.append(_normalise(_dot(jnp.exp2(s - m).astype(BF16), v_ref[i, :, hs])))
        _store_head_pairs(rs, o_ref.at[i])


def _mla_ctx(qc, kc, vc):
    out = pl.pallas_call(
        _mla_ctx_kernel,
        grid=(N_CTX_B,),
        in_specs=[_ctx_seq_spec(512, 1), _ctx_seq_spec(512, 1), _ctx_seq_spec(512, 1)],
        out_specs=pl.BlockSpec((1, CTX_L, 256), lambda g: (g, 0, 0)),
        out_shape=jax.ShapeDtypeStruct((N_CTX_B, CTX_L, 256), BF16),
        compiler_params=_params(("parallel",)),
        name="mla_ctx",
    )(_by_seq(qc), _by_seq(kc), _by_seq(vc))
    return out.reshape(T_CTX, 256)


def _mla_lat_kernel(q_ref, k_ref, v_ref, kx_ref, vx_ref, o_ref):
    rs = []
    for h in range(C_HEADS):
        hs = slice(h * LANES, (h + 1) * LANES)
        q = q_ref[:, hs]
        s_lat = _dot_nt(q, k_ref[:, hs])
        s_ctx = _dot_nt(q, kx_ref[:, hs])
        m = jnp.max(jnp.concatenate([s_lat, s_ctx], axis=1), axis=-1, keepdims=True)
        o = (_dot(jnp.exp2(s_lat - m).astype(BF16), v_ref[:, hs])
             + _dot(jnp.exp2(s_ctx - m).astype(BF16), vx_ref[:, hs]))
        rs.append(_normalise(o))
    _store_head_pairs(rs, o_ref)


def _mla_lat(qc, kc, vc, kx_l, vx_l):
    nq = LAT_L // TQ_C
    qbase = 0
    kbase = 0
    return pl.pallas_call(
        _mla_lat_kernel,
        grid=(N_LAT_B, nq),
        in_specs=[
            pl.BlockSpec((TQ_C, 512), lambda b, n: (qbase + b * nq + n, 0)),
            pl.BlockSpec((LAT_L, 512), lambda b, n: (kbase + b, 0)),
            pl.BlockSpec((LAT_L, 512), lambda b, n: (kbase + b, 0)),
            pl.BlockSpec((None, CTX_L, 512), lambda b, n: (b, 0, 0)),
            pl.BlockSpec((None, CTX_L, 512), lambda b, n: (b, 0, 0)),
        ],
        out_specs=pl.BlockSpec((TQ_C, 256), lambda b, n: (b * nq + n, 0)),
        out_shape=jax.ShapeDtypeStruct((T_LAT, 256), BF16),
        compiler_params=_params(("parallel", "arbitrary")),
        name="mla_lat",
    )(qc, kc, vc, kx_l, vx_l)


def _tile_scan(s_ref, hb_ref, tab_ref, d, row0, carry, reverse):
    n16 = SSM_CHUNK // 16

    def body(i, carry):
        blk = (n16 - 1 - i) if reverse else i
        r = pl.multiple_of(row0 + blk * 16, 16)
        halves = [None, None]
        for half in ((1, 0) if reverse else (0, 1)):
            x = s_ref[pl.ds(pl.multiple_of(r + half * 8, 8), 8), :]
            xr, xi = x[:, 0:SSM_N], x[:, SSM_N:2 * SSM_N]
            for ki, k in enumerate((1, 2, 4)):
                ar = tab_ref[d, ki, :, 0:SSM_N]
                ai = tab_ref[d, ki, :, SSM_N:2 * SSM_N]
                shift = (8 - k) if reverse else k
                sr = pltpu.roll(xr, shift, 0)
                si = pltpu.roll(xi, shift, 0)
                xr, xi = xr + (ar * sr - ai * si), xi + (ar * si + ai * sr)
            pr = tab_ref[d, 3, :, 0:SSM_N]
            pi = tab_ref[d, 3, :, SSM_N:2 * SSM_N]
            cr = jnp.broadcast_to(carry[:, 0:SSM_N], (8, SSM_N))
            ci = jnp.broadcast_to(carry[:, SSM_N:2 * SSM_N], (8, SSM_N))
            xr, xi = xr + (pr * cr - pi * ci), xi + (pr * ci + pi * cr)
            last = 0 if reverse else 7
            carry = jnp.concatenate([xr[last:last + 1], xi[last:last + 1]], axis=1)
            halves[half] = jnp.concatenate([xr, xi], axis=1)
        hb_ref[pl.ds(r, 16), :] = jnp.concatenate(halves, axis=0).astype(BF16)
        return carry

    return lax.fori_loop(0, n16, body, carry)


def _ssm_kernel(uf_ref, ub_ref, bmat_ref, cmat_ref, tab_ref, h0_ref, yf_ref, yb_ref, fin_ref,
                s_ref, hb_ref, carry_ref):
    nseq = uf_ref.shape[0]
    rows = SSM_GROUP * SSM_CHUNK
    step = 2 * SSM_CHUNK

    @pl.when(pl.program_id(1) == 0)
    def _():
        carry_ref[...] = h0_ref[...]

    for d, u_ref, y_ref in ((0, uf_ref, yf_ref), (1, ub_ref, yb_ref)):
        for g0 in range(0, nseq, SSM_GROUP):
            u_bf = u_ref[g0:g0 + SSM_GROUP].reshape(rows, SSM_W).astype(BF16)
            for r in range(0, rows, step):
                s_ref[r:r + step, :] = _dot(u_bf[r:r + step], bmat_ref[d])
            for b in range(SSM_GROUP):
                carry = carry_ref[d, g0 + b:g0 + b + 1, :]
                carry = _tile_scan(s_ref, hb_ref, tab_ref, d, b * SSM_CHUNK, carry, d == 1)
                carry_ref[d, g0 + b:g0 + b + 1, :] = carry
            for r in range(0, rows, step):
                y = _dot(hb_ref[r:r + step, :], cmat_ref[d])
                for b in range(step // SSM_CHUNK):
                    y_ref[g0 + r // SSM_CHUNK + b] = y[b * SSM_CHUNK:(b + 1) * SSM_CHUNK]
    fin_ref[...] = carry_ref[...]


def _ssm(u_fwd_spec, u_bwd_spec, y_fwd_spec, y_bwd_spec, y_shape, grid, nseq, nseq_total, name):
    state = pl.BlockSpec((2, nseq, 2 * SSM_N), lambda g, c: (0, g, 0))
    rows = SSM_GROUP * SSM_CHUNK
    return pl.pallas_call(
        _ssm_kernel,
        grid=grid,
        in_specs=[u_fwd_spec, u_bwd_spec, _full((2, SSM_W, 2 * SSM_N)), _full((2, 2 * SSM_N, SSM_W)),
                  _full((2, 4, 8, 2 * SSM_N)), state],
        out_specs=[y_fwd_spec, y_bwd_spec, state],
        out_shape=[jax.ShapeDtypeStruct(y_shape, F32), jax.ShapeDtypeStruct(y_shape, F32),
                   jax.ShapeDtypeStruct((2, nseq_total, 2 * SSM_N), F32)],
        scratch_shapes=[pltpu.VMEM((rows, 2 * SSM_N), F32), pltpu.VMEM((rows, 2 * SSM_N), BF16),
                        pltpu.VMEM((2, nseq, 2 * SSM_N), F32)],
        compiler_params=_params(("parallel", "arbitrary")),
        name=name,
    )


def _ssm_ctx(u, bmat, cmat, tab, h0):
    nseq = 8
    blk = pl.BlockSpec((nseq, CTX_L, SSM_W), lambda g, c: (LAT_SEQ_BLOCKS // nseq + g, 0, 0))
    out = pl.BlockSpec((nseq, CTX_L, SSM_W), lambda g, c: (g, 0, 0))
    call = _ssm(blk, blk, out, out, (N_CTX_B, CTX_L, SSM_W), (N_CTX_B // nseq, 1), nseq, N_CTX_B, "ssm_ctx")
    u3 = u.reshape(T_ALL // CTX_L, CTX_L, SSM_W)
    return call(u3, u3, bmat, cmat, tab, h0)


def _ssm_lat(u, bmat, cmat, tab, h0):
    nc = LAT_L // SSM_CHUNK
    fwd = pl.BlockSpec((N_LAT_B, None, SSM_CHUNK, SSM_W), lambda g, c: (0, c, 0, 0))
    bwd = pl.BlockSpec((N_LAT_B, None, SSM_CHUNK, SSM_W), lambda g, c: (0, nc - 1 - c, 0, 0))
    call = _ssm(fwd, bwd, fwd, bwd, (N_LAT_B, nc, SSM_CHUNK, SSM_W), (1, nc), N_LAT_B, N_LAT_B, "ssm_lat")
    u4 = u.reshape(T_ALL // LAT_L, nc, SSM_CHUNK, SSM_W)
    return call(u4, u4, bmat, cmat, tab, h0)


def _gelu_tanh(x):
    return 0.5 * x * (1.0 + jnp.tanh(math.sqrt(2.0 / math.pi) * (x + 0.044715 * (x * x * x))))


def _outproj_kernel(x_ref, u_ref, oal_ref, oac_ref, ocl_ref, occ_ref, yfl_ref, yfc_ref, ybl_ref, ybc_ref,
                    mod_ref, d_ref, wglu_ref, woa_ref, wob_ref, woc_ref, gffn_ref, wrhi_ref, wrlo_ref, rb_ref,
                    x1_ref, h2_ref, comb_ref):
    is_lat = pl.program_id(0) < N_LAT_TILES
    oa = jnp.where(is_lat, oal_ref[...], oac_ref[...])
    oc = jnp.where(is_lat, ocl_ref[...], occ_ref[...])
    y = d_ref[...] * u_ref[...] + jnp.where(is_lat, yfl_ref[...] + ybl_ref[...], yfc_ref[...] + ybc_ref[...])
    g = _gelu_tanh(y)
    ob = g * jax.nn.sigmoid(_dot(g.astype(BF16), wglu_ref[...]))
    mix = _dot(oa, woa_ref[...]) + _dot(ob.astype(BF16), wob_ref[...]) + _dot(oc, woc_ref[...])
    x1 = x_ref[...] + mod_ref[2:3, :] * mix
    x1_ref[...] = x1
    ms = jnp.mean(x1 * x1, axis=-1, keepdims=True)
    h2 = x1 * lax.rsqrt(ms + EPS) * gffn_ref[...]
    h2 = h2 * (1.0 + mod_ref[4:5, :]) + mod_ref[3:4, :]
    h2_hi, h2_lo = _split_bf16(h2)
    h2_ref[...] = h2_hi

    lt = _dot_nt(wrhi_ref[...], h2_hi) + _dot_nt(wrhi_ref[...], h2_lo) + _dot_nt(wrlo_ref[...], h2_hi)
    sc = [jax.nn.sigmoid(lt[e:e + 1, :]) for e in range(N_EXP)]
    bi = [sc[e] + rb_ref[e:e + 1, 0:1] for e in range(N_EXP)]
    gs = []
    for gi in range(N_GRP):
        v = bi[gi * GRP_SZ:(gi + 1) * GRP_SZ]
        best2 = None
        for i in range(GRP_SZ):
            for j in range(i + 1, GRP_SZ):
                pair = v[i] + v[j]
                best2 = pair if best2 is None else jnp.maximum(best2, pair)
        gs.append(best2)
    best_g = jnp.zeros((1, TM), jnp.int32)
    best_v = gs[0]
    for gi in range(1, N_GRP):
        upd = gs[gi] > best_v
        best_g = jnp.where(upd, gi, best_g)
        best_v = jnp.where(upd, gs[gi], best_v)
    wts = []
    for e in range(N_EXP):
        gi = e // GRP_SZ
        beaten = jnp.zeros((1, TM), F32)
        for j in range(gi * GRP_SZ, (gi + 1) * GRP_SZ):
            if j == e:
                continue
            if j < e:
                beaten = beaten + jnp.where(bi[j] >= bi[e], 1.0, 0.0)
            else:
                beaten = beaten + jnp.where(bi[j] > bi[e], 1.0, 0.0)
        keep = jnp.where(best_g == gi, jnp.where(beaten < 1.5, 1.0, 0.0), 0.0)
        wts.append(keep * sc[e])
    tot = wts[0]
    for e in range(1, N_EXP):
        tot = tot + wts[e]
    inv = 1.0 / tot
    comb_t = jnp.concatenate([w_ * inv for w_ in wts] + [best_g.astype(F32)]
                             + [jnp.zeros((LANES - N_EXP - 1, TM), F32)], axis=0)
    comb_ref[...] = comb_t.T


def _outproj(x, u, oa_lat, oa_ctx, oc_lat, oc_ctx, yf_lat, yf_ctx, yb_lat, yb_ctx, mod_l, w, wr_hi, wr_lo, rb):
    nt = T_ALL // TM
    tile = lambda width: pl.BlockSpec((TM, width), lambda i: (i, 0))
    lat = lambda width: pl.BlockSpec((TM, width), lambda i: (jnp.minimum(i, N_LAT_TILES - 1), 0))
    ctx = lambda width: pl.BlockSpec((TM, width), lambda i: (jnp.maximum(i - N_LAT_TILES, 0), 0))
    in_specs = [
        tile(D), tile(256), lat(512), ctx(512), lat(256), ctx(256), lat(256), ctx(256), lat(256), ctx(256),
        pl.BlockSpec((None, 6, D), lambda i: (_mod_row(i), 0, 0)),
        _full((1, SSM_W)), _full((SSM_W, SSM_W)),
        _full((512, D)), _full((256, D)), _full((256, D)), _full((1, D)),
        _full((LANES, D)), _full((LANES, D)), _full((N_EXP, LANES)),
    ]
    return pl.pallas_call(
        _outproj_kernel,
        grid=(nt,),
        in_specs=in_specs,
        out_specs=[tile(D), tile(D), tile(LANES)],
        out_shape=[jax.ShapeDtypeStruct((T_ALL, D), F32), jax.ShapeDtypeStruct((T_ALL, D), BF16),
                   jax.ShapeDtypeStruct((T_ALL, LANES), F32)],
        compiler_params=_params(("parallel",)),
        name="outproj_router",
    )(x, u, oa_lat, oa_ctx, oc_lat, oc_ctx, yf_lat, yf_ctx, yb_lat, yb_ctx, mod_l, w["ssm_d"], w["wglu"],
      w["woa"], w["wob"], w["woc"], w["gffn"], wr_hi, wr_lo, rb)


def _moe_route(h_ref, comb_ref, xs_ref, ws_ref, ys_ref, pt_ref, seg_ref):
    lane = lax.broadcasted_iota(jnp.int32, (TM_MOE, LANES), 1)
    comb = comb_ref[...]
    grp = jnp.sum(jnp.where(lane == N_EXP, comb, 0.0), axis=-1, keepdims=True).astype(jnp.int32)
    onehot = jnp.where(lane == grp, 1.0, 0.0)
    tok_r = lax.broadcasted_iota(jnp.int32, (TM_MOE, TM_MOE), 0)
    tok_c = lax.broadcasted_iota(jnp.int32, (TM_MOE, TM_MOE), 1)
    earlier = jnp.where(tok_c < tok_r, 1.0, 0.0).astype(BF16)
    rank = _dot(earlier, onehot.astype(BF16))
    count = jnp.sum(onehot, axis=0, keepdims=True).astype(jnp.int32)
    padded = lax.shift_left(lax.shift_right_logical(count + (MOE_CHUNK - 1), MOE_SHIFT), MOE_SHIFT)
    lane1 = lax.broadcasted_iota(jnp.int32, (1, LANES), 1)
    base = jnp.zeros((1, LANES), jnp.int32)
    run = jnp.zeros((1, 1), jnp.int32)
    for g in range(N_GRP):
        base = jnp.where(lane1 == g, run, base)
        seg_ref[2 * g] = lax.shift_right_logical(run, MOE_SHIFT)[0, 0]
        seg_ref[2 * g + 1] = lax.shift_right_logical(padded[:, g:g + 1], MOE_SHIFT)[0, 0]
        run = run + padded[:, g:g + 1]
    slot = jnp.sum(onehot * (base.astype(F32) + rank), axis=-1, keepdims=True)
    slot_lane = lax.broadcasted_iota(jnp.int32, (TM_MOE, MOE_SLOTS), 1)
    pt_ref[...] = jnp.where(slot_lane == slot.astype(jnp.int32), 1.0, 0.0).astype(BF16)
    slot_row = jnp.broadcast_to(slot, (TM_MOE, LANES)).T[0:1, :].astype(jnp.int32)

    c1 = comb.astype(BF16).astype(F32)
    c2 = (comb - c1).astype(BF16).astype(F32)
    c3 = comb - c1 - c2
    pieces = (c1 + pltpu.roll(c2, 32, 1) + pltpu.roll(c3, 64, 1)).astype(BF16)
    x = h_ref[...]
    step = 512
    for r0 in range(0, MOE_SLOTS, step):
        slot_sub = lax.broadcasted_iota(jnp.int32, (step, TM_MOE), 0) + r0
        p = jnp.where(slot_sub == slot_row, 1.0, 0.0).astype(BF16)
        xs_ref[r0:r0 + step, :] = _dot(p, x).astype(BF16)
        wp = _dot(p, pieces)
        ws_ref[r0:r0 + step, :] = wp + pltpu.roll(wp, LANES - 32, 1) + pltpu.roll(wp, LANES - 64, 1)
    ys_ref[...] = jnp.zeros_like(ys_ref)


def _moe_kernel(h_ref, comb_ref, x1_ref, mod_ref, wg_ref, wu_ref, wd_ref, *refs, split):
    outs, (xs_ref, ws_ref, ys_ref, pt_ref, seg_ref) = refs[:-5], refs[-5:]
    e = pl.program_id(1)

    @pl.when(e == 0)
    def _():
        _moe_route(h_ref, comb_ref, xs_ref, ws_ref, ys_ref, pt_ref, seg_ref)

    g = e // GRP_SZ
    first = seg_ref[2 * g]
    count = seg_ref[2 * g + 1]

    def run(chunk, nrows):
        r = pl.multiple_of(chunk * MOE_CHUNK, MOE_CHUNK)
        xc = xs_ref[pl.ds(r, nrows), :]
        lane = lax.broadcasted_iota(jnp.int32, (nrows, LANES), 1)
        wcol = jnp.sum(jnp.where(lane == e, ws_ref[pl.ds(r, nrows), :], 0.0), axis=-1, keepdims=True)
        y = None
        for h0 in range(0, D_EXP, D_EXP // 2):
            hs = slice(h0, h0 + D_EXP // 2)
            a = _dot(xc, wg_ref[:, hs])
            b = _dot(xc, wu_ref[:, hs])
            act = ((a * jax.nn.sigmoid(a)) * b).astype(BF16)
            part = _dot(act, wd_ref[hs, :])
            y = part if y is None else y + part
        ys_ref[pl.ds(r, nrows), :] += wcol * y

    def triple(c, carry):
        run(first + 3 * c, 3 * MOE_CHUNK)
        return carry

    triples = count // 3
    lax.fori_loop(0, triples, triple, 0)
    left = count - 3 * triples

    @pl.when(left == 2)
    def _():
        run(first + count - 2, 2 * MOE_CHUNK)

    @pl.when(left == 1)
    def _():
        run(first + count - 1, MOE_CHUNK)

    def scatter(o_ref):
        ysb = ys_ref[...].astype(BF16)
        step = 256
        for r0 in range(0, TM_MOE, step):
            y = _dot(pt_ref[r0:r0 + step, :], ysb)
            o_ref[r0:r0 + step, :] = x1_ref[r0:r0 + step, :] + mod_ref[5:6, :] * y

    last = e == N_EXP - 1
    if split:
        is_lat = pl.program_id(0) < T_LAT // TM_MOE
        pl.when(last & is_lat)(lambda: scatter(outs[0]))
        pl.when(last & jnp.logical_not(is_lat))(lambda: scatter(outs[1]))
    else:
        pl.when(last)(lambda: scatter(outs[0]))


def _moe(h2, comb, x1, mod_l, wg, wu, wd, split=False):
    nt = T_ALL // TM_MOE
    nl = T_LAT // TM_MOE
    per = TM_MOE // TM
    tile = lambda width: pl.BlockSpec((TM_MOE, width), lambda i, e: (i, 0))
    if split:
        out_specs = [pl.BlockSpec((TM_MOE, D), lambda i, e: (jnp.minimum(i, nl - 1), 0)),
                     pl.BlockSpec((TM_MOE, D), lambda i, e: (jnp.maximum(i - nl, 0), 0))]
        out_shape = [jax.ShapeDtypeStruct((T_LAT, D), F32), jax.ShapeDtypeStruct((T_CTX, D), F32)]
    else:
        out_specs = tile(D)
        out_shape = jax.ShapeDtypeStruct((T_ALL, D), F32)
    return pl.pallas_call(
        functools.partial(_moe_kernel, split=split),
        grid=(nt, N_EXP),
        in_specs=[
            tile(D), tile(LANES), tile(D),
            pl.BlockSpec((None, 6, D), lambda i, e: (_mod_row(i * per), 0, 0)),
            pl.BlockSpec((None, D, D_EXP), lambda i, e: (e, 0, 0)),
            pl.BlockSpec((None, D, D_EXP), lambda i, e: (e, 0, 0)),
            pl.BlockSpec((None, D_EXP, D), lambda i, e: (e, 0, 0)),
        ],
        out_specs=out_specs,
        out_shape=out_shape,
        scratch_shapes=[pltpu.VMEM((MOE_SLOTS, D), BF16), pltpu.VMEM((MOE_SLOTS, LANES), F32),
                        pltpu.VMEM((MOE_SLOTS, D), F32), pltpu.VMEM((TM_MOE, MOE_SLOTS), BF16),
                        pltpu.SMEM((2 * N_GRP,), jnp.int32)],
        compiler_params=pltpu.CompilerParams(dimension_semantics=("arbitrary", "arbitrary"),
                                             vmem_limit_bytes=MOE_VMEM_LIMIT),
        name="moe",
    )(h2, comb, x1, mod_l, wg, wu, wd)


def _pad_heads(w, heads, dim):
    k = w.shape[0]
    return jnp.pad(w.reshape(k, heads, dim), ((0, 0), (0, 0), (0, LANES - dim))).reshape(k, heads * LANES)


def _block_diag_b(bb_t):
    eye = jnp.eye(SSM_G, dtype=bb_t.dtype)
    return jnp.einsum("ghp,gk->ghkp", bb_t, eye).reshape(SSM_W, SSM_N)


def _block_diag_c(c):
    eye = jnp.eye(SSM_G, dtype=c.dtype)
    return jnp.einsum("ghp,gk->gpkh", c, eye).reshape(SSM_N, SSM_W)


def _scan_tables(pw_re, pw_im):
    def flat(p):
        return jnp.transpose(p, (0, 2, 1, 3)).reshape(2, 8, SSM_N)

    pw = jnp.concatenate([flat(pw_re), flat(pw_im)], axis=-1)
    pos = jnp.arange(8)[:, None]
    tabs = []
    for d in range(2):
        rows = []
        for k in (1, 2, 4):
            keep = (pos >= k) if d == 0 else (pos <= 7 - k)
            rows.append(jnp.where(keep, pw[d, k - 1][None, :], 0.0))
        rows.append(pw[d] if d == 0 else pw[d, ::-1])
        tabs.append(jnp.stack(rows))
    return jnp.stack(tabs)


def kernel(x_prompt, x_sample, cache_attn_k, cache_attn_v, cache_mla_ckv, cache_mla_kpe, state_ssm_re,
           state_ssm_im, c, c_ctx, norm_mix, norm_ffn, w_ada, b_ada, w_in, a_q_norm, a_k_norm, a_sink,
           ssm_a_re, ssm_a_im, ssm_log_dt, ssm_b_re, ssm_b_im, ssm_c_re, ssm_c_im, ssm_d, w_glu,
           mla_q_norm, mla_kv_norm, w_q_b, w_kv_b, mla_qh_norm, mla_kh_norm, w_out, w_router,
           router_bias, w_gate, w_up, w_down):
    x = jnp.concatenate([x_sample.reshape(T_LAT, D), x_prompt.reshape(T_CTX, D)], axis=0)

    cond = jnp.concatenate([c_ctx[None, :], c, jnp.zeros((16 - 1 - N_LAT_B, D), F32)], axis=0)
    mod = _adaln(cond, w_ada, b_ada)

    pw_re, pw_im, bbt_re, bbt_im = _discretize(ssm_a_re, ssm_a_im, ssm_log_dt, ssm_b_re, ssm_b_im)
    pw_re = pw_re.reshape(DEPTH, 2, SSM_G, 8, SSM_P)
    pw_im = pw_im.reshape(DEPTH, 2, SSM_G, 8, SSM_P)
    bbt_re = bbt_re.reshape(DEPTH, 2, SSM_G, SSM_H, SSM_P)
    bbt_im = bbt_im.reshape(DEPTH, 2, SSM_G, SSM_H, SSM_P)
    h0_ctx = jnp.zeros((2, N_CTX_B, 2 * SSM_N), F32)

    tab_a, tab_c = _rope_tables()

    wr_t = jnp.pad(w_router.T, ((0, LANES - N_EXP), (0, 0)))
    wr_hi = wr_t.astype(BF16)
    wr_lo = (wr_t - wr_hi.astype(F32)).astype(BF16)
    rb = jnp.broadcast_to(router_bias[:, None], (N_EXP, LANES))

    w_kv4 = w_kv_b.reshape(DEPTH, C_KVLORA, C_HEADS, C_NOPE + C_V)
    wkn_all = jnp.pad(w_kv4[..., :C_NOPE], ((0, 0), (0, 0), (0, 0), (0, LANES - C_NOPE))
                      ).reshape(DEPTH, C_KVLORA, C_HEADS * LANES).astype(BF16)
    wv_all = w_kv4[..., C_NOPE:].reshape(DEPTH, C_KVLORA, C_HEADS * C_V).astype(BF16)
    gkh_all = jnp.pad(mla_kh_norm, ((0, 0), (0, LANES - C_QK))).reshape(DEPTH, 1, LANES)

    kpe_slab = jnp.pad(cache_mla_kpe, ((0, 0), (0, 0), (0, 0), (C_NOPE, LANES - C_QK)))
    kx_all, vx_all = _cache_kv(cache_mla_ckv, kpe_slab, wkn_all, wv_all, gkh_all)

    cache_k = cache_attn_k.reshape(N_LAT_B, DEPTH, CTX_L, A_KV * A_DH)
    cache_v = cache_attn_v.reshape(N_LAT_B, DEPTH, CTX_L, A_KV * A_DH)

    ks, vs, ckvs, kpes, fins = [], [], [], [], []
    for l in range(DEPTH):
        wl = w_in[l]
        w = {
            "gmix": norm_mix[l][None, :],
            "wq": wl[:, 0:512].astype(BF16),
            "wkv": wl[:, 512:768].astype(BF16),
            "wu": wl[:, 768:1024].astype(BF16),
            "wcq": wl[:, 1024:1280].astype(BF16),
            "wt": jnp.pad(wl[:, 1280:1440], ((0, 0), (0, 256 - 160))).astype(BF16),
            "gq": jnp.tile(a_q_norm[l], 2)[None, :],
            "gk": jnp.tile(a_k_norm[l], 2)[None, :],
            "gcq": mla_q_norm[l][None, :],
            "gckv": mla_kv_norm[l][None, :],
            "wqb": _pad_heads(w_q_b[l], C_HEADS, C_QK).astype(BF16),
            "wkn": wkn_all[l],
            "wv": wv_all[l],
            "gqh": jnp.pad(mla_qh_norm[l], (0, LANES - C_QK))[None, :],
            "gkh": gkh_all[l],
            "ssm_d": ssm_d[l].reshape(1, SSM_W),
            "wglu": w_glu[l].astype(BF16),
            "woa": w_out[l, 0:512].astype(BF16),
            "wob": w_out[l, 512:768].astype(BF16),
            "woc": w_out[l, 768:1024].astype(BF16),
            "gffn": norm_ffn[l][None, :],
        }
        mod_l = mod[l]
        qa, kab, vaug, u, qc, kc, vc, ka_ctx, va_ctx, ckv_ctx, kpe_ctx = _proj(x, mod_l, w, tab_a, tab_c)

        sink = jnp.broadcast_to(a_sink[l][:, None], (A_HEADS, LANES))
        oa_ctx = _attn_a_ctx(qa, kab, vaug, sink)
        oa_lat = _attn_a_lat(qa, kab, vaug, cache_k[:, l], cache_v[:, l], sink)
        oc_ctx = _mla_ctx(qc, kc, vc)
        oc_lat = _mla_lat(qc, kc, vc, kx_all[l], vx_all[l])

        bmat = jnp.stack([jnp.concatenate([_block_diag_b(bbt_re[l, d]), _block_diag_b(bbt_im[l, d])], axis=1)
                          for d in range(2)]).astype(BF16)
        cmat = jnp.stack([jnp.concatenate([_block_diag_c(ssm_c_re[l, d]), -_block_diag_c(ssm_c_im[l, d])], axis=0)
                          for d in range(2)]).astype(BF16)
        tab = _scan_tables(pw_re[l], pw_im[l])
        yf_ctx, yb_ctx, fin = _ssm_ctx(u, bmat, cmat, tab, h0_ctx)
        h0 = jnp.swapaxes(jnp.concatenate([state_ssm_re[:, l].reshape(N_LAT_B, 2, SSM_N),
                                            state_ssm_im[:, l].reshape(N_LAT_B, 2, SSM_N)], axis=-1), 0, 1)
        yf_lat, yb_lat, _ = _ssm_lat(u, bmat, cmat, tab, h0)

        x1, h2, comb = _outproj(x, u, oa_lat, oa_ctx, oc_lat, oc_ctx,
                                yf_lat.reshape(T_LAT, SSM_W), yf_ctx.reshape(T_CTX, SSM_W),
                                yb_lat.reshape(T_LAT, SSM_W), yb_ctx.reshape(T_CTX, SSM_W),
                                mod_l, w, wr_hi, wr_lo, rb)
        x = _moe(h2, comb, x1, mod_l, w_gate[l].astype(BF16), w_up[l].astype(BF16), w_down[l].astype(BF16),
                 split=(l == DEPTH - 1))

        ks.append(ka_ctx.reshape(N_CTX_B, CTX_L, A_KV, A_DH))
        vs.append(va_ctx.reshape(N_CTX_B, CTX_L, A_KV, A_DH))
        ckvs.append(ckv_ctx.reshape(N_CTX_B, CTX_L, C_KVLORA))
        kpes.append(kpe_ctx.reshape(N_CTX_B, CTX_L, C_ROPE))
        fins.append(fin)

    y_sample = x[0].reshape(N_LAT_B, LAT_L, D)
    y_prompt = x[1].reshape(N_CTX_B, CTX_L, D)
    fin = jnp.transpose(jnp.stack(fins, axis=0), (2, 0, 1, 3))
    new_re = fin[..., 0:SSM_N].reshape(N_CTX_B, DEPTH, 2, SSM_G, SSM_P)
    new_im = fin[..., SSM_N:].reshape(N_CTX_B, DEPTH, 2, SSM_G, SSM_P)
    return (y_prompt, y_sample, jnp.stack(ks, axis=1), jnp.stack(vs, axis=1), jnp.stack(ckvs, axis=1),
            jnp.stack(kpes, axis=1), new_re, new_im)
```

```python
import functools
import math

import jax
import jax.numpy as jnp
from jax import lax
from jax.experimental import pallas as pl
from jax.experimental.pallas import tpu as pltpu

F32 = jnp.float32
BF16 = jnp.bfloat16

D = 1024
N_CTX_B, CTX_L = 32, 256
N_LAT_B, LAT_L = 4, 4096
DEPTH = 4
GRID_W = 64
EPS = 1e-6
ROPE_BASE = 10000.0
A_HEADS, A_KV, A_DH = 8, 2, 64
SSM_G, SSM_H, SSM_P = 16, 16, 64
SSM_W = SSM_G * SSM_H
SSM_N = SSM_G * SSM_P
SSM_SLABS = SSM_N // 128
C_HEADS, C_NOPE, C_ROPE, C_V = 4, 64, 32, 64
C_QK = C_NOPE + C_ROPE
C_QLORA, C_KVLORA = 256, 128
N_EXP, N_GRP, GRP_SZ, D_EXP = 16, 4, 4, 512
WINDOW = 128

T_CTX = N_CTX_B * CTX_L
T_LAT = N_LAT_B * LAT_L
T_ALL = T_CTX + T_LAT

LANES = 128
HALF = LANES // 2
VMEM_LIMIT = 48 * 1024 * 1024

TM = 512
N_LAT_TILES = T_LAT // TM
LAT_TILES_PER_B = LAT_L // TM
LAT_SEQ_BLOCKS = T_LAT // CTX_L
TQ_A = 256
WIN_KEYS = TQ_A + 2 * WINDOW + CTX_L
TQ_C = 256
CTX_ATTN_SEQS = 4
TM_MOE = 1024
MOE_SHIFT = 7
MOE_CHUNK = 1 << MOE_SHIFT
MOE_SLOTS = TM_MOE + N_GRP * MOE_CHUNK
MOE_VMEM_LIMIT = 56 * 1024 * 1024
SSM_CHUNK = 256
SSM_GROUP = 4
NEG = -1e30
LOG2E = math.log2(math.e)


def _dot(a, b):
    return jnp.dot(a, b, preferred_element_type=F32)


def _dot_nt(a, b):
    return lax.dot_general(a, b, (((1,), (1,)), ((), ())), preferred_element_type=F32)


def _split_bf16(x):
    hi = x.astype(BF16)
    lo = (x - hi.astype(F32)).astype(BF16)
    return hi, lo


def _params(sem):
    return pltpu.CompilerParams(dimension_semantics=sem, vmem_limit_bytes=VMEM_LIMIT)


def _full(shape):
    n = len(shape)
    return pl.BlockSpec(shape, lambda *_: (0,) * n)


def _mod_row(i):
    return jnp.where(i < N_LAT_TILES, 1 + i // LAT_TILES_PER_B, 0)


def _pos_block(i):
    return jnp.where(i < N_LAT_TILES, i % LAT_TILES_PER_B, LAT_TILES_PER_B)


def _adaln_kernel(c_ref, w_ref, b_ref, o_ref):
    c = c_ref[...]
    s = c * jax.nn.sigmoid(c)
    s_hi, s_lo = _split_bf16(s)
    w_hi, w_lo = _split_bf16(w_ref[...])
    o_ref[...] = _dot(s_hi, w_hi) + _dot(s_hi, w_lo) + _dot(s_lo, w_hi) + b_ref[...]


def _adaln(cond, w_ada, b_ada):
    nt = 6
    out = pl.pallas_call(
        _adaln_kernel,
        grid=(DEPTH, nt),
        in_specs=[
            pl.BlockSpec((16, D), lambda l, n: (0, 0)),
            pl.BlockSpec((None, D, D), lambda l, n: (l, 0, n)),
            pl.BlockSpec((None, 1, D), lambda l, n: (l, 0, n)),
        ],
        out_specs=pl.BlockSpec((None, 16, D), lambda l, n: (l, 0, n)),
        out_shape=jax.ShapeDtypeStruct((DEPTH, 16, nt * D), F32),
        compiler_params=_params(("parallel", "parallel")),
        name="adaln",
    )(cond, w_ada, b_ada.reshape(DEPTH, 1, nt * D))
    return out.reshape(DEPTH, 16, nt, D)


def _disc_kernel(are_ref, aim_ref, ldt_ref, bre_ref, bim_ref, pwre_ref, pwim_ref, bbre_ref, bbim_ref):
    a_re = are_ref[...]
    a_im = aim_ref[...]
    dt = jnp.exp(ldt_ref[...])
    mag = jnp.exp(a_re * dt)
    ab_re = mag * jnp.cos(a_im * dt)
    ab_im = mag * jnp.sin(a_im * dt)
    den = a_re * a_re + a_im * a_im
    n_re = ab_re - 1.0
    k_re = (n_re * a_re + ab_im * a_im) / den
    k_im = (ab_im * a_re - n_re * a_im) / den
    b_re = bre_ref[...]
    b_im = bim_ref[...]
    bbre_ref[...] = k_re * b_re - k_im * b_im
    bbim_ref[...] = k_re * b_im + k_im * b_re
    p_re, p_im = ab_re, ab_im
    for k in range(8):
        pwre_ref[:, k:k + 1, :] = p_re
        pwim_ref[:, k:k + 1, :] = p_im
        p_re, p_im = p_re * ab_re - p_im * ab_im, p_re * ab_im + p_im * ab_re


def _discretize(ssm_a_re, ssm_a_im, ssm_log_dt, ssm_b_re, ssm_b_im):
    n = DEPTH * 2 * SSM_G
    a_re = ssm_a_re.reshape(n, 1, SSM_P)
    a_im = ssm_a_im.reshape(n, 1, SSM_P)
    ldt = jnp.broadcast_to(ssm_log_dt.reshape(n, 1, 1), (n, 1, SSM_P))
    b_re = jnp.swapaxes(ssm_b_re.reshape(n, SSM_P, SSM_H), 1, 2)
    b_im = jnp.swapaxes(ssm_b_im.reshape(n, SSM_P, SSM_H), 1, 2)
    sa = jax.ShapeDtypeStruct((n, 8, SSM_P), F32)
    sb = jax.ShapeDtypeStruct((n, SSM_H, SSM_P), F32)
    return pl.pallas_call(
        _disc_kernel,
        out_shape=(sa, sa, sb, sb),
        compiler_params=pltpu.CompilerParams(vmem_limit_bytes=VMEM_LIMIT),
        name="s5_discretize",
    )(a_re, a_im, ldt, b_re, b_im)


def _rope_tables():
    t = jnp.arange(LAT_L)
    row = (t // GRID_W).astype(F32)[:, None]
    col = (t % GRID_W).astype(F32)[:, None]
    lane = jnp.arange(LANES)

    def build(d, width, valid):
        half = width // 2
        q = half // 2
        on_col = d >= half
        dd = jnp.where(on_col, d - half, d)
        first = dd < q
        j = jnp.where(first, dd, dd - q).astype(F32)
        inv = ROPE_BASE ** (-j / q)
        ang = jnp.where(on_col[None, :], col, row) * inv[None, :]
        cos = jnp.where(valid[None, :], jnp.cos(ang), 1.0)
        sin = jnp.where(valid[None, :], jnp.sin(ang), 0.0)
        s_up = jnp.where(first[None, :], -sin, 0.0)
        s_dn = jnp.where(first[None, :], 0.0, sin)
        tab = jnp.concatenate([cos, s_up, s_dn], axis=1)
        ident = jnp.concatenate([jnp.ones((TM, LANES), F32), jnp.zeros((TM, 2 * LANES), F32)], axis=1)
        return jnp.concatenate([tab, ident], axis=0)

    tab_a = build(lane % A_DH, A_DH, jnp.ones((LANES,), bool))
    dc = jnp.clip(lane - C_NOPE, 0, C_ROPE - 1)
    tab_c = build(dc, C_ROPE, (lane >= C_NOPE) & (lane < C_QK))
    return tab_a, tab_c


def _rope(x, tab_ref, shift):
    cos = tab_ref[:, 0:LANES]
    s_up = tab_ref[:, LANES:2 * LANES]
    s_dn = tab_ref[:, 2 * LANES:3 * LANES]
    return x * cos + pltpu.roll(x, LANES - shift, 1) * s_up + pltpu.roll(x, shift, 1) * s_dn


def _half_norm(xs, gain):
    lo = lax.broadcasted_iota(jnp.int32, xs.shape, 1) < A_DH
    sq = xs * xs
    s_lo = jnp.sum(jnp.where(lo, sq, 0.0), axis=-1, keepdims=True)
    s_hi = jnp.sum(jnp.where(lo, 0.0, sq), axis=-1, keepdims=True)
    r = jnp.where(lo, lax.rsqrt(s_lo * (1.0 / A_DH) + EPS), lax.rsqrt(s_hi * (1.0 / A_DH) + EPS))
    return xs * r * gain


def _with_ones(v, heads):
    lo = lax.broadcasted_iota(jnp.int32, (v.shape[0], LANES), 1) < HALF
    slabs = []
    for h in range(heads):
        pair = v[:, (h // 2) * LANES:(h // 2 + 1) * LANES]
        own = pair if h % 2 == 0 else pltpu.roll(pair, HALF, 1)
        slabs.append(jnp.where(lo, own, 1.0).astype(BF16))
    return slabs


def _normalise(o):
    return o / pltpu.roll(o, HALF, 1)


def _store_head_pairs(rs, o_ref):
    lo = lax.broadcasted_iota(jnp.int32, rs[0].shape, 1) < HALF
    for j in range(len(rs) // 2):
        slab = jnp.where(lo, rs[2 * j], pltpu.roll(rs[2 * j + 1], HALF, 1))
        o_ref[:, j * LANES:(j + 1) * LANES] = slab.astype(o_ref.dtype)


def _mla_keys_values(ckv_n, kpe_slab, wkn_ref, wv_ref, gkh, tab_ref):
    cb = ckv_n.astype(BF16)
    kn = _dot(cb, wkn_ref[...])
    v = _dot(cb, wv_ref[...])
    ks = []
    for h in range(C_HEADS):
        slab = kn[:, h * LANES:(h + 1) * LANES] + kpe_slab
        ss = jnp.sum(slab * slab, axis=-1, keepdims=True)
        slab = slab * lax.rsqrt(ss * (1.0 / C_QK) + EPS) * gkh
        if tab_ref is not None:
            slab = _rope(slab, tab_ref, C_ROPE // 4)
        ks.append(slab.astype(BF16))
    return ks, v


def _proj_kernel(x_ref, mod_ref, gmix_ref, wq_ref, wkv_ref, wu_ref, wcq_ref, wt_ref,
                 gq_ref, gk_ref, gcq_ref, gckv_ref, wqb_ref, wkn_ref, wv_ref, gqh_ref, gkh_ref,
                 ta_ref, tc_ref,
                 qa_ref, kab_ref, vaug_ref, u_ref, qc_ref, kc_ref, vc_ref, ka_ref, va_ref, ckv_ref, kpe_ref):
    x = x_ref[...]
    sh = mod_ref[0:1, :]
    sc = mod_ref[1:2, :]
    ms = jnp.mean(x * x, axis=-1, keepdims=True)
    h = x * lax.rsqrt(ms + EPS) * gmix_ref[...]
    h = (h * (1.0 + sc) + sh).astype(BF16)

    lo = lax.broadcasted_iota(jnp.int32, (TM, LANES), 1) < A_DH

    p_q = _dot(h, wq_ref[...])
    for j in range(A_HEADS // 2):
        slab = _half_norm(p_q[:, j * LANES:(j + 1) * LANES], gq_ref[...])
        slab = _rope(slab, ta_ref, A_DH // 4) * (A_DH ** -0.5 * LOG2E)
        swapped = pltpu.roll(slab, A_DH, 1)
        kv_head = (2 * j) // (A_HEADS // A_KV)
        if kv_head == 0:
            q0 = jnp.where(lo, slab, 0.0)
            q1 = jnp.where(lo, swapped, 0.0)
        else:
            q0 = jnp.where(lo, 0.0, swapped)
            q1 = jnp.where(lo, 0.0, slab)
        qa_ref[:, (2 * j) * LANES:(2 * j + 1) * LANES] = q0.astype(BF16)
        qa_ref[:, (2 * j + 1) * LANES:(2 * j + 2) * LANES] = q1.astype(BF16)

    p_kv = _dot(h, wkv_ref[...])
    k = _rope(_half_norm(p_kv[:, 0:LANES], gk_ref[...]), ta_ref, A_DH // 4)
    ka_ref[...] = k
    kab_ref[...] = k.astype(BF16)
    va_ref[...] = p_kv[:, LANES:2 * LANES]
    for kvh, slab in enumerate(_with_ones(p_kv[:, LANES:2 * LANES], A_KV)):
        vaug_ref[:, kvh * LANES:(kvh + 1) * LANES] = slab

    u_ref[...] = _dot(h, wu_ref[...])

    p_cq = _dot(h, wcq_ref[...])
    ms = jnp.mean(p_cq * p_cq, axis=-1, keepdims=True)
    cq = (p_cq * lax.rsqrt(ms + EPS) * gcq_ref[...]).astype(BF16)
    q = _dot(cq, wqb_ref[...])
    for hd in range(C_HEADS):
        slab = q[:, hd * LANES:(hd + 1) * LANES]
        ss = jnp.sum(slab * slab, axis=-1, keepdims=True)
        slab = slab * lax.rsqrt(ss * (1.0 / C_QK) + EPS) * gqh_ref[...]
        slab = _rope(slab, tc_ref, C_ROPE // 4) * (C_QK ** -0.5 * LOG2E)
        qc_ref[:, hd * LANES:(hd + 1) * LANES] = slab.astype(BF16)

    p_t = _dot(h, wt_ref[...])
    ckv = p_t[:, 0:LANES]
    ms = jnp.mean(ckv * ckv, axis=-1, keepdims=True)
    ckv_n = ckv * lax.rsqrt(ms + EPS) * gckv_ref[...]
    ckv_ref[...] = ckv_n
    kpe_wide = p_t[:, LANES:2 * LANES]
    kpe_ref[...] = kpe_wide[:, 0:C_ROPE]
    kpe_slab = pltpu.roll(kpe_wide, C_NOPE, 1)
    ks, v = _mla_keys_values(ckv_n, kpe_slab, wkn_ref, wv_ref, gkh_ref[...], tc_ref)
    for hd, slab in enumerate(_with_ones(v, C_HEADS)):
        kc_ref[:, hd * LANES:(hd + 1) * LANES] = ks[hd]
        vc_ref[:, hd * LANES:(hd + 1) * LANES] = slab


def _proj(x, mod_l, w, tab_a, tab_c):
    nt = T_ALL // TM
    tile = lambda width: pl.BlockSpec((TM, width), lambda i: (i, 0))
    tab = pl.BlockSpec((TM, 3 * LANES), lambda i: (_pos_block(i), 0))
    in_specs = [
        tile(D),
        pl.BlockSpec((None, 6, D), lambda i: (_mod_row(i), 0, 0)),
        _full((1, D)),
        _full((D, 512)), _full((D, 256)), _full((D, 256)), _full((D, 256)), _full((D, 256)),
        _full((1, LANES)), _full((1, LANES)), _full((1, 256)), _full((1, LANES)),
        _full((256, 512)), _full((LANES, 512)), _full((LANES, 256)),
        _full((1, LANES)), _full((1, LANES)),
        tab, tab,
    ]
    ctx = lambda width: pl.BlockSpec((TM, width), lambda i: (jnp.maximum(i - N_LAT_TILES, 0), 0))
    outs = [(1024, BF16), (128, BF16), (256, BF16), (256, F32), (512, BF16), (512, BF16), (512, BF16)]
    ctx_outs = [128, 128, 128, C_ROPE]
    return pl.pallas_call(
        _proj_kernel,
        grid=(nt,),
        in_specs=in_specs,
        out_specs=[tile(wd) for wd, _ in outs] + [ctx(wd) for wd in ctx_outs],
        out_shape=([jax.ShapeDtypeStruct((T_ALL, wd), dt) for wd, dt in outs]
                   + [jax.ShapeDtypeStruct((T_CTX, wd), F32) for wd in ctx_outs]),
        compiler_params=_params(("arbitrary",)),
        name="proj",
    )(x, mod_l, w["gmix"], w["wq"], w["wkv"], w["wu"], w["wcq"], w["wt"],
      w["gq"], w["gk"], w["gcq"], w["gckv"], w["wqb"], w["wkn"], w["wv"], w["gqh"], w["gkh"],
      tab_a, tab_c)


def _cache_kv_kernel(ckv_ref, kpe_ref, wkn_ref, wv_ref, gkh_ref, k_ref, v_ref):
    ks, v = _mla_keys_values(ckv_ref[...], kpe_ref[...], wkn_ref, wv_ref, gkh_ref[...], None)
    for hd, slab in enumerate(_with_ones(v, C_HEADS)):
        k_ref[:, hd * LANES:(hd + 1) * LANES] = ks[hd]
        v_ref[:, hd * LANES:(hd + 1) * LANES] = slab


def _cache_kv(cache_ckv, cache_kpe_slab, wkn, wv, gkh):
    return pl.pallas_call(
        _cache_kv_kernel,
        grid=(DEPTH, N_LAT_B),
        in_specs=[
            pl.BlockSpec((None, None, CTX_L, LANES), lambda l, b: (b, l, 0, 0)),
            pl.BlockSpec((None, None, CTX_L, LANES), lambda l, b: (b, l, 0, 0)),
            pl.BlockSpec((None, LANES, 512), lambda l, b: (l, 0, 0)),
            pl.BlockSpec((None, LANES, 256), lambda l, b: (l, 0, 0)),
            pl.BlockSpec((None, 1, LANES), lambda l, b: (l, 0, 0)),
        ],
        out_specs=[
            pl.BlockSpec((None, None, CTX_L, 512), lambda l, b: (l, b, 0, 0)),
            pl.BlockSpec((None, None, CTX_L, 512), lambda l, b: (l, b, 0, 0)),
        ],
        out_shape=[jax.ShapeDtypeStruct((DEPTH, N_LAT_B, CTX_L, 512), BF16),
                   jax.ShapeDtypeStruct((DEPTH, N_LAT_B, CTX_L, 512), BF16)],
        compiler_params=_params(("parallel", "parallel")),
        name="mla_cache_kv",
    )(cache_ckv, cache_kpe_slab, wkn, wv, gkh)


def _sink_column(sink_ref, kv_head, rows):
    grp = A_HEADS // A_KV
    return jnp.concatenate(
        [jnp.broadcast_to(sink_ref[kv_head * grp + g:kv_head * grp + g + 1, 0:1] * LOG2E, (rows, 1))
         for g in range(grp)],
        axis=0)


def _sink_softmax_head(q_ref, kv_head, rows, kb, vslab, bias, sink_ref):
    grp = A_HEADS // A_KV
    q = jnp.concatenate([q_ref[:, (kv_head * grp + g) * LANES:(kv_head * grp + g + 1) * LANES] for g in range(grp)],
                        axis=0)
    s = _dot_nt(q, kb)
    if bias is not None:
        s = s + jnp.concatenate([bias] * grp, axis=0)
    sk = _sink_column(sink_ref, kv_head, rows)
    m = jnp.maximum(jnp.max(s, axis=-1, keepdims=True), sk)
    o = _dot(jnp.exp2(s - m).astype(BF16), vslab)
    lo = lax.broadcasted_iota(jnp.int32, o.shape, 1) < HALF
    o = o + jnp.where(lo, 0.0, jnp.exp2(sk - m))
    r = _normalise(o)
    return [r[g * rows:(g + 1) * rows] for g in range(grp)]


def _attn_a_ctx_kernel(q_ref, k_ref, v_ref, sink_ref, o_ref):
    for i in range(CTX_ATTN_SEQS):
        kb = k_ref[i]
        rs = []
        for kvh in range(A_KV):
            rs.extend(_sink_softmax_head(q_ref.at[i], kvh, CTX_L, kb, v_ref[i, :, kvh * LANES:(kvh + 1) * LANES],
                                         None, sink_ref))
        _store_head_pairs(rs, o_ref.at[i])


def _ctx_seq_spec(width, nseq=CTX_ATTN_SEQS):
    return pl.BlockSpec((nseq, CTX_L, width), lambda g: (LAT_SEQ_BLOCKS // nseq + g, 0, 0))


def _by_seq(a):
    return a.reshape(T_ALL // CTX_L, CTX_L, a.shape[-1])


def _attn_a_ctx(qa, ka, va, sink):
    out = pl.pallas_call(
        _attn_a_ctx_kernel,
        grid=(N_CTX_B // CTX_ATTN_SEQS,),
        in_specs=[_ctx_seq_spec(1024), _ctx_seq_spec(LANES), _ctx_seq_spec(A_KV * LANES), _full((A_HEADS, LANES))],
        out_specs=pl.BlockSpec((CTX_ATTN_SEQS, CTX_L, 512), lambda g: (g, 0, 0)),
        out_shape=jax.ShapeDtypeStruct((N_CTX_B, CTX_L, 512), BF16),
        compiler_params=_params(("parallel",)),
        name="attn_a_ctx",
    )(_by_seq(qa), _by_seq(ka), _by_seq(va), sink)
    return out.reshape(T_CTX, 512)


def _attn_a_lat_kernel(q_ref, kp_ref, kc_ref, kn_ref, vp_ref, vc_ref, vn_ref, kx_ref, vx_ref, sink_ref, band_ref,
                       o_ref):
    grp = A_HEADS // A_KV
    n = pl.program_id(1)
    nb = LAT_L // TQ_A
    kb = jnp.concatenate([kp_ref[...], kc_ref[...], kn_ref[...], kx_ref[...].astype(BF16)], axis=0)
    vx = _with_ones(vx_ref[...], A_KV)
    kj = lax.broadcasted_iota(jnp.int32, (1, WIN_KEYS), 1)
    outside = ((kj < WINDOW) & (n == 0)) | ((kj >= WINDOW + TQ_A) & (kj < 2 * WINDOW + TQ_A) & (n == nb - 1))
    bias = band_ref[...] + jnp.where(outside, NEG, 0.0)
    rs = []
    for kvh in range(A_KV):
        hs = slice(kvh * LANES, (kvh + 1) * LANES)
        vslab = jnp.concatenate([vp_ref[:, hs], vc_ref[:, hs], vn_ref[:, hs], vx[kvh]], axis=0)
        rs.extend(_sink_softmax_head(q_ref, kvh, TQ_A, kb, vslab, bias, sink_ref))
    _store_head_pairs(rs, o_ref)


def _attn_a_lat(qa, ka, va, cache_k_l, cache_v_l, sink):
    nb = LAT_L // TQ_A
    base = 0

    per = TQ_A // WINDOW
    nw = LAT_L // WINDOW
    own = lambda width: pl.BlockSpec((TQ_A, width), lambda b, n: (base + b * nb + n, 0))
    prev = lambda width: pl.BlockSpec((WINDOW, width), lambda b, n: (b * nw + jnp.maximum(n * per - 1, 0), 0))
    nxt = lambda width: pl.BlockSpec((WINDOW, width), lambda b, n: (b * nw + jnp.minimum((n + 1) * per, nw - 1), 0))
    cache = pl.BlockSpec((None, CTX_L, LANES), lambda b, n: (b, 0, 0))
    vw = A_KV * LANES
    return pl.pallas_call(
        _attn_a_lat_kernel,
        grid=(N_LAT_B, nb),
        in_specs=[
            own(1024),
            prev(LANES), own(LANES), nxt(LANES), prev(vw), own(vw), nxt(vw),
            cache, cache, _full((A_HEADS, LANES)), _full((TQ_A, WIN_KEYS)),
        ],
        out_specs=pl.BlockSpec((TQ_A, 512), lambda b, n: (b * nb + n, 0)),
        out_shape=jax.ShapeDtypeStruct((T_LAT, 512), BF16),
        compiler_params=_params(("parallel", "parallel")),
        name="attn_a_lat",
    )(qa, ka, ka, ka, va, va, va, cache_k_l, cache_v_l, sink, _band_mask())


def _band_mask():
    qi = jnp.arange(TQ_A)[:, None]
    kj = jnp.arange(WIN_KEYS)[None, :]
    ok = (jnp.abs(kj - WINDOW - qi) <= WINDOW) | (kj >= TQ_A + 2 * WINDOW)
    return jnp.where(ok, 0.0, NEG).astype(F32)


def _mla_ctx_kernel(q_ref, k_ref, v_ref, o_ref):
    for i in range(q_ref.shape[0]):
        rs = []
        for h in range(C_HEADS):
            hs = slice(h * LANES, (h + 1) * LANES)
            s = _dot_nt(q_ref[i, :, hs], k_ref[i, :, hs])
            m = jnp.max(s, axis=-1, keepdims=True)
            rs.append(_normalise(_dot(jnp.exp2(s - m).astype(BF16), v_ref[i, :, hs])))
        _store_head_pairs(rs, o_ref.at[i])


def _mla_ctx(qc, kc, vc):
    out = pl.pallas_call(
        _mla_ctx_kernel,
        grid=(N_CTX_B,),
        in_specs=[_ctx_seq_spec(512, 1), _ctx_seq_spec(512, 1), _ctx_seq_spec(512, 1)],
        out_specs=pl.BlockSpec((1, CTX_L, 256), lambda g: (g, 0, 0)),
        out_shape=jax.ShapeDtypeStruct((N_CTX_B, CTX_L, 256), BF16),
        compiler_params=_params(("parallel",)),
        name="mla_ctx",
    )(_by_seq(qc), _by_seq(kc), _by_seq(vc))
    return out.reshape(T_CTX, 256)


def _mla_lat_kernel(q_ref, k_ref, v_ref, kx_ref, vx_ref, o_ref):
    rs = []
    for h in range(C_HEADS):
        hs = slice(h * LANES, (h + 1) * LANES)
        q = q_ref[:, hs]
        s_lat = _dot_nt(q, k_ref[:, hs])
        s_ctx = _dot_nt(q, kx_ref[:, hs])
        m = jnp.max(jnp.concatenate([s_lat, s_ctx], axis=1), axis=-1, keepdims=True)
        o = (_dot(jnp.exp2(s_lat - m).astype(BF16), v_ref[:, hs])
             + _dot(jnp.exp2(s_ctx - m).astype(BF16), vx_ref[:, hs]))
        rs.append(_normalise(o))
    _store_head_pairs(rs, o_ref)


def _mla_lat(qc, kc, vc, kx_l, vx_l):
    nq = LAT_L // TQ_C
    qbase = 0
    kbase = 0
    return pl.pallas_call(
        _mla_lat_kernel,
        grid=(N_LAT_B, nq),
        in_specs=[
            pl.BlockSpec((TQ_C, 512), lambda b, n: (qbase + b * nq + n, 0)),
            pl.BlockSpec((LAT_L, 512), lambda b, n: (kbase + b, 0)),
            pl.BlockSpec((LAT_L, 512), lambda b, n: (kbase + b, 0)),
            pl.BlockSpec((None, CTX_L, 512), lambda b, n: (b, 0, 0)),
            pl.BlockSpec((None, CTX_L, 512), lambda b, n: (b, 0, 0)),
        ],
        out_specs=pl.BlockSpec((TQ_C, 256), lambda b, n: (b * nq + n, 0)),
        out_shape=jax.ShapeDtypeStruct((T_LAT, 256), BF16),
        compiler_params=_params(("parallel", "arbitrary")),
        name="mla_lat",
    )(qc, kc, vc, kx_l, vx_l)


def _tile_scan(s_ref, hb_ref, tab_ref, d, row0, carry, reverse):
    n16 = SSM_CHUNK // 16

    def body(i, carry):
        blk = (n16 - 1 - i) if reverse else i
        r = pl.multiple_of(row0 + blk * 16, 16)
        halves = [None, None]
        for half in ((1, 0) if reverse else (0, 1)):
            x = s_ref[pl.ds(pl.multiple_of(r + half * 8, 8), 8), :]
            xr, xi = x[:, 0:SSM_N], x[:, SSM_N:2 * SSM_N]
            for ki, k in enumerate((1, 2, 4)):
                ar = tab_ref[d, ki, :, 0:SSM_N]
                ai = tab_ref[d, ki, :, SSM_N:2 * SSM_N]
                shift = (8 - k) if reverse else k
                sr = pltpu.roll(xr, shift, 0)
                si = pltpu.roll(xi, shift, 0)
                xr, xi = xr + (ar * sr - ai * si), xi + (ar * si + ai * sr)
            pr = tab_ref[d, 3, :, 0:SSM_N]
            pi = tab_ref[d, 3, :, SSM_N:2 * SSM_N]
            cr = jnp.broadcast_to(carry[:, 0:SSM_N], (8, SSM_N))
            ci = jnp.broadcast_to(carry[:, SSM_N:2 * SSM_N], (8, SSM_N))
            xr, xi = xr + (pr * cr - pi * ci), xi + (pr * ci + pi * cr)
            last = 0 if reverse else 7
            carry = jnp.concatenate([xr[last:last + 1], xi[last:last + 1]], axis=1)
            halves[half] = jnp.concatenate([xr, xi], axis=1)
        hb_ref[pl.ds(r, 16), :] = jnp.concatenate(halves, axis=0).astype(BF16)
        return carry

    return lax.fori_loop(0, n16, body, carry)


def _ssm_kernel(uf_ref, ub_ref, bmat_ref, cmat_ref, tab_ref, h0_ref, yf_ref, yb_ref, fin_ref,
                s_ref, hb_ref, carry_ref):
    nseq = uf_ref.shape[0]
    rows = SSM_GROUP * SSM_CHUNK
    step = 2 * SSM_CHUNK

    @pl.when(pl.program_id(1) == 0)
    def _():
        carry_ref[...] = h0_ref[...]

    for d, u_ref, y_ref in ((0, uf_ref, yf_ref), (1, ub_ref, yb_ref)):
        for g0 in range(0, nseq, SSM_GROUP):
            u_bf = u_ref[g0:g0 + SSM_GROUP].reshape(rows, SSM_W).astype(BF16)
            for r in range(0, rows, step):
                s_ref[r:r + step, :] = _dot(u_bf[r:r + step], bmat_ref[d])
            for b in range(SSM_GROUP):
                carry = carry_ref[d, g0 + b:g0 + b + 1, :]
                carry = _tile_scan(s_ref, hb_ref, tab_ref, d, b * SSM_CHUNK, carry, d == 1)
                carry_ref[d, g0 + b:g0 + b + 1, :] = carry
            for r in range(0, rows, step):
                y = _dot(hb_ref[r:r + step, :], cmat_ref[d])
                for b in range(step // SSM_CHUNK):
                    y_ref[g0 + r // SSM_CHUNK + b] = y[b * SSM_CHUNK:(b + 1) * SSM_CHUNK]
    fin_ref[...] = carry_ref[...]


def _ssm(u_fwd_spec, u_bwd_spec, y_fwd_spec, y_bwd_spec, y_shape, grid, nseq, nseq_total, name):
    state = pl.BlockSpec((2, nseq, 2 * SSM_N), lambda g, c: (0, g, 0))
    rows = SSM_GROUP * SSM_CHUNK
    return pl.pallas_call(
        _ssm_kernel,
        grid=grid,
        in_specs=[u_fwd_spec, u_bwd_spec, _full((2, SSM_W, 2 * SSM_N)), _full((2, 2 * SSM_N, SSM_W)),
                  _full((2, 4, 8, 2 * SSM_N)), state],
        out_specs=[y_fwd_spec, y_bwd_spec, state],
        out_shape=[jax.ShapeDtypeStruct(y_shape, F32), jax.ShapeDtypeStruct(y_shape, F32),
                   jax.ShapeDtypeStruct((2, nseq_total, 2 * SSM_N), F32)],
        scratch_shapes=[pltpu.VMEM((rows, 2 * SSM_N), F32), pltpu.VMEM((rows, 2 * SSM_N), BF16),
                        pltpu.VMEM((2, nseq, 2 * SSM_N), F32)],
        compiler_params=_params(("parallel", "arbitrary")),
        name=name,
    )


def _ssm_ctx(u, bmat, cmat, tab, h0):
    nseq = 8
    blk = pl.BlockSpec((nseq, CTX_L, SSM_W), lambda g, c: (LAT_SEQ_BLOCKS // nseq + g, 0, 0))
    out = pl.BlockSpec((nseq, CTX_L, SSM_W), lambda g, c: (g, 0, 0))
    call = _ssm(blk, blk, out, out, (N_CTX_B, CTX_L, SSM_W), (N_CTX_B // nseq, 1), nseq, N_CTX_B, "ssm_ctx")
    u3 = u.reshape(T_ALL // CTX_L, CTX_L, SSM_W)
    return call(u3, u3, bmat, cmat, tab, h0)


def _ssm_lat(u, bmat, cmat, tab, h0):
    nc = LAT_L // SSM_CHUNK
    fwd = pl.BlockSpec((N_LAT_B, None, SSM_CHUNK, SSM_W), lambda g, c: (0, c, 0, 0))
    bwd = pl.BlockSpec((N_LAT_B, None, SSM_CHUNK, SSM_W), lambda g, c: (0, nc - 1 - c, 0, 0))
    call = _ssm(fwd, bwd, fwd, bwd, (N_LAT_B, nc, SSM_CHUNK, SSM_W), (1, nc), N_LAT_B, N_LAT_B, "ssm_lat")
    u4 = u.reshape(T_ALL // LAT_L, nc, SSM_CHUNK, SSM_W)
    return call(u4, u4, bmat, cmat, tab, h0)


def _gelu_tanh(x):
    return 0.5 * x * (1.0 + jnp.tanh(math.sqrt(2.0 / math.pi) * (x + 0.044715 * (x * x * x))))


def _outproj_kernel(x_ref, u_ref, oal_ref, oac_ref, ocl_ref, occ_ref, yfl_ref, yfc_ref, ybl_ref, ybc_ref,
                    mod_ref, d_ref, wglu_ref, woa_ref, wob_ref, woc_ref, gffn_ref, wrhi_ref, wrlo_ref, rb_ref,
                    x1_ref, h2_ref, comb_ref):
    is_lat = pl.program_id(0) < N_LAT_TILES
    oa = jnp.where(is_lat, oal_ref[...], oac_ref[...])
    oc = jnp.where(is_lat, ocl_ref[...], occ_ref[...])
    y = d_ref[...] * u_ref[...] + jnp.where(is_lat, yfl_ref[...] + ybl_ref[...], yfc_ref[...] + ybc_ref[...])
    g = _gelu_tanh(y)
    ob = g * jax.nn.sigmoid(_dot(g.astype(BF16), wglu_ref[...]))
    mix = _dot(oa, woa_ref[...]) + _dot(ob.astype(BF16), wob_ref[...]) + _dot(oc, woc_ref[...])
    x1 = x_ref[...] + mod_ref[2:3, :] * mix
    x1_ref[...] = x1
    ms = jnp.mean(x1 * x1, axis=-1, keepdims=True)
    h2 = x1 * lax.rsqrt(ms + EPS) * gffn_ref[...]
    h2 = h2 * (1.0 + mod_ref[4:5, :]) + mod_ref[3:4, :]
    h2_hi, h2_lo = _split_bf16(h2)
    h2_ref[...] = h2_hi

    lt = _dot_nt(wrhi_ref[...], h2_hi) + _dot_nt(wrhi_ref[...], h2_lo) + _dot_nt(wrlo_ref[...], h2_hi)
    sc = [jax.nn.sigmoid(lt[e:e + 1, :]) for e in range(N_EXP)]
    bi = [sc[e] + rb_ref[e:e + 1, 0:1] for e in range(N_EXP)]
    gs = []
    for gi in range(N_GRP):
        v = bi[gi * GRP_SZ:(gi + 1) * GRP_SZ]
        best2 = None
        for i in range(GRP_SZ):
            for j in range(i + 1, GRP_SZ):
                pair = v[i] + v[j]
                best2 = pair if best2 is None else jnp.maximum(best2, pair)
        gs.append(best2)
    best_g = jnp.zeros((1, TM), jnp.int32)
    best_v = gs[0]
    for gi in range(1, N_GRP):
        upd = gs[gi] > best_v
        best_g = jnp.where(upd, gi, best_g)
        best_v = jnp.where(upd, gs[gi], best_v)
    wts = []
    for e in range(N_EXP):
        gi = e // GRP_SZ
        beaten = jnp.zeros((1, TM), F32)
        for j in range(gi * GRP_SZ, (gi + 1) * GRP_SZ):
            if j == e:
                continue
            if j < e:
                beaten = beaten + jnp.where(bi[j] >= bi[e], 1.0, 0.0)
            else:
                beaten = beaten + jnp.where(bi[j] > bi[e], 1.0, 0.0)
        keep = jnp.where(best_g == gi, jnp.where(beaten < 1.5, 1.0, 0.0), 0.0)
        wts.append(keep * sc[e])
    tot = wts[0]
    for e in range(1, N_EXP):
        tot = tot + wts[e]
    inv = 1.0 / tot
    comb_t = jnp.concatenate([w_ * inv for w_ in wts] + [best_g.astype(F32)]
                             + [jnp.zeros((LANES - N_EXP - 1, TM), F32)], axis=0)
    comb_ref[...] = comb_t.T


def _outproj(x, u, oa_lat, oa_ctx, oc_lat, oc_ctx, yf_lat, yf_ctx, yb_lat, yb_ctx, mod_l, w, wr_hi, wr_lo, rb):
    nt = T_ALL // TM
    tile = lambda width: pl.BlockSpec((TM, width), lambda i: (i, 0))
    lat = lambda width: pl.BlockSpec((TM, width), lambda i: (jnp.minimum(i, N_LAT_TILES - 1), 0))
    ctx = lambda width: pl.BlockSpec((TM, width), lambda i: (jnp.maximum(i - N_LAT_TILES, 0), 0))
    in_specs = [
        tile(D), tile(256), lat(512), ctx(512), lat(256), ctx(256), lat(256), ctx(256), lat(256), ctx(256),
        pl.BlockSpec((None, 6, D), lambda i: (_mod_row(i), 0, 0)),
        _full((1, SSM_W)), _full((SSM_W, SSM_W)),
        _full((512, D)), _full((256, D)), _full((256, D)), _full((1, D)),
        _full((LANES, D)), _full((LANES, D)), _full((N_EXP, LANES)),
    ]
    return pl.pallas_call(
        _outproj_kernel,
        grid=(nt,),
        in_specs=in_specs,
        out_specs=[tile(D), tile(D), tile(LANES)],
        out_shape=[jax.ShapeDtypeStruct((T_ALL, D), F32), jax.ShapeDtypeStruct((T_ALL, D), BF16),
                   jax.ShapeDtypeStruct((T_ALL, LANES), F32)],
        compiler_params=_params(("parallel",)),
        name="outproj_router",
    )(x, u, oa_lat, oa_ctx, oc_lat, oc_ctx, yf_lat, yf_ctx, yb_lat, yb_ctx, mod_l, w["ssm_d"], w["wglu"],
      w["woa"], w["wob"], w["woc"], w["gffn"], wr_hi, wr_lo, rb)


def _moe_route(h_ref, comb_ref, xs_ref, ws_ref, ys_ref, pt_ref, seg_ref):
    lane = lax.broadcasted_iota(jnp.int32, (TM_MOE, LANES), 1)
    comb = comb_ref[...]
    grp = jnp.sum(jnp.where(lane == N_EXP, comb, 0.0), axis=-1, keepdims=True).astype(jnp.int32)
    onehot = jnp.where(lane == grp, 1.0, 0.0)
    tok_r = lax.broadcasted_iota(jnp.int32, (TM_MOE, TM_MOE), 0)
    tok_c = lax.broadcasted_iota(jnp.int32, (TM_MOE, TM_MOE), 1)
    earlier = jnp.where(tok_c < tok_r, 1.0, 0.0).astype(BF16)
    rank = _dot(earlier, onehot.astype(BF16))
    count = jnp.sum(onehot, axis=0, keepdims=True).astype(jnp.int32)
    padded = lax.shift_left(lax.shift_right_logical(count + (MOE_CHUNK - 1), MOE_SHIFT), MOE_SHIFT)
    lane1 = lax.broadcasted_iota(jnp.int32, (1, LANES), 1)
    base = jnp.zeros((1, LANES), jnp.int32)
    run = jnp.zeros((1, 1), jnp.int32)
    for g in range(N_GRP):
        base = jnp.where(lane1 == g, run, base)
        seg_ref[2 * g] = lax.shift_right_logical(run, MOE_SHIFT)[0, 0]
        seg_ref[2 * g + 1] = lax.shift_right_logical(padded[:, g:g + 1], MOE_SHIFT)[0, 0]
        run = run + padded[:, g:g + 1]
    slot = jnp.sum(onehot * (base.astype(F32) + rank), axis=-1, keepdims=True)
    slot_lane = lax.broadcasted_iota(jnp.int32, (TM_MOE, MOE_SLOTS), 1)
    pt_ref[...] = jnp.where(slot_lane == slot.astype(jnp.int32), 1.0, 0.0).astype(BF16)
    slot_row = jnp.broadcast_to(slot, (TM_MOE, LANES)).T[0:1, :].astype(jnp.int32)

    c1 = comb.astype(BF16).astype(F32)
    c2 = (comb - c1).astype(BF16).astype(F32)
    c3 = comb - c1 - c2
    pieces = (c1 + pltpu.roll(c2, 32, 1) + pltpu.roll(c3, 64, 1)).astype(BF16)
    x = h_ref[...]
    step = 512
    for r0 in range(0, MOE_SLOTS, step):
        slot_sub = lax.broadcasted_iota(jnp.int32, (step, TM_MOE), 0) + r0
        p = jnp.where(slot_sub == slot_row, 1.0, 0.0).astype(BF16)
        xs_ref[r0:r0 + step, :] = _dot(p, x).astype(BF16)
        wp = _dot(p, pieces)
        ws_ref[r0:r0 + step, :] = wp + pltpu.roll(wp, LANES - 32, 1) + pltpu.roll(wp, LANES - 64, 1)
    ys_ref[...] = jnp.zeros_like(ys_ref)


def _moe_kernel(h_ref, comb_ref, x1_ref, mod_ref, wg_ref, wu_ref, wd_ref, *refs, split):
    outs, (xs_ref, ws_ref, ys_ref, pt_ref, seg_ref) = refs[:-5], refs[-5:]
    e = pl.program_id(1)

    @pl.when(e == 0)
    def _():
        _moe_route(h_ref, comb_ref, xs_ref, ws_ref, ys_ref, pt_ref, seg_ref)

    g = e // GRP_SZ
    first = seg_ref[2 * g]
    count = seg_ref[2 * g + 1]

    def run(chunk, nrows):
        r = pl.multiple_of(chunk * MOE_CHUNK, MOE_CHUNK)
        xc = xs_ref[pl.ds(r, nrows), :]
        lane = lax.broadcasted_iota(jnp.int32, (nrows, LANES), 1)
        wcol = jnp.sum(jnp.where(lane == e, ws_ref[pl.ds(r, nrows), :], 0.0), axis=-1, keepdims=True)
        y = None
        for h0 in range(0, D_EXP, D_EXP // 2):
            hs = slice(h0, h0 + D_EXP // 2)
            a = _dot(xc, wg_ref[:, hs])
            b = _dot(xc, wu_ref[:, hs])
            act = ((a * jax.nn.sigmoid(a)) * b).astype(BF16)
            part = _dot(act, wd_ref[hs, :])
            y = part if y is None else y + part
        ys_ref[pl.ds(r, nrows), :] += wcol * y

    def triple(c, carry):
        run(first + 3 * c, 3 * MOE_CHUNK)
        return carry

    triples = count // 3
    lax.fori_loop(0, triples, triple, 0)
    left = count - 3 * triples

    @pl.when(left == 2)
    def _():
        run(first + count - 2, 2 * MOE_CHUNK)

    @pl.when(left == 1)
    def _():
        run(first + count - 1, MOE_CHUNK)

    def scatter(o_ref):
        ysb = ys_ref[...].astype(BF16)
        step = 256
        for r0 in range(0, TM_MOE, step):
            y = _dot(pt_ref[r0:r0 + step, :], ysb)
            o_ref[r0:r0 + step, :] = x1_ref[r0:r0 + step, :] + mod_ref[5:6, :] * y

    last = e == N_EXP - 1
    if split:
        is_lat = pl.program_id(0) < T_LAT // TM_MOE
        pl.when(last & is_lat)(lambda: scatter(outs[0]))
        pl.when(last & jnp.logical_not(is_lat))(lambda: scatter(outs[1]))
    else:
        pl.when(last)(lambda: scatter(outs[0]))


def _moe(h2, comb, x1, mod_l, wg, wu, wd, layer, split=False):
    nt = T_ALL // TM_MOE
    nl = T_LAT // TM_MOE
    per = TM_MOE // TM
    tile = lambda width: pl.BlockSpec((TM_MOE, width), lambda i, e: (i, 0))
    if split:
        out_specs = [pl.BlockSpec((TM_MOE, D), lambda i, e: (jnp.minimum(i, nl - 1), 0)),
                     pl.BlockSpec((TM_MOE, D), lambda i, e: (jnp.maximum(i - nl, 0), 0))]
        out_shape = [jax.ShapeDtypeStruct((T_LAT, D), F32), jax.ShapeDtypeStruct((T_CTX, D), F32)]
    else:
        out_specs = tile(D)
        out_shape = jax.ShapeDtypeStruct((T_ALL, D), F32)
    return pl.pallas_call(
        functools.partial(_moe_kernel, split=split),
        grid=(nt, N_EXP),
        in_specs=[
            tile(D), tile(LANES), tile(D),
            pl.BlockSpec((None, 6, D), lambda i, e: (_mod_row(i * per), 0, 0)),
            pl.BlockSpec((None, None, D, D_EXP), lambda i, e: (layer, e, 0, 0)),
            pl.BlockSpec((None, None, D, D_EXP), lambda i, e: (layer, e, 0, 0)),
            pl.BlockSpec((None, None, D_EXP, D), lambda i, e: (layer, e, 0, 0)),
        ],
        out_specs=out_specs,
        out_shape=out_shape,
        scratch_shapes=[pltpu.VMEM((MOE_SLOTS, D), BF16), pltpu.VMEM((MOE_SLOTS, LANES), F32),
                        pltpu.VMEM((MOE_SLOTS, D), F32), pltpu.VMEM((TM_MOE, MOE_SLOTS), BF16),
                        pltpu.SMEM((2 * N_GRP,), jnp.int32)],
        compiler_params=pltpu.CompilerParams(dimension_semantics=("arbitrary", "arbitrary"),
                                             vmem_limit_bytes=MOE_VMEM_LIMIT),
        name="moe",
    )(h2, comb, x1, mod_l, wg, wu, wd)


def _pad_heads(w, heads, dim):
    k = w.shape[0]
    return jnp.pad(w.reshape(k, heads, dim), ((0, 0), (0, 0), (0, LANES - dim))).reshape(k, heads * LANES)


def _block_diag_b(bb_t):
    eye = jnp.eye(SSM_G, dtype=bb_t.dtype)
    return jnp.einsum("ghp,gk->ghkp", bb_t, eye).reshape(SSM_W, SSM_N)


def _block_diag_c(c):
    eye = jnp.eye(SSM_G, dtype=c.dtype)
    return jnp.einsum("ghp,gk->gpkh", c, eye).reshape(SSM_N, SSM_W)


def _scan_tables(pw_re, pw_im):
    def flat(p):
        return jnp.transpose(p, (0, 2, 1, 3)).reshape(2, 8, SSM_N)

    pw = jnp.concatenate([flat(pw_re), flat(pw_im)], axis=-1)
    pos = jnp.arange(8)[:, None]
    tabs = []
    for d in range(2):
        rows = []
        for k in (1, 2, 4):
            keep = (pos >= k) if d == 0 else (pos <= 7 - k)
            rows.append(jnp.where(keep, pw[d, k - 1][None, :], 0.0))
        rows.append(pw[d] if d == 0 else pw[d, ::-1])
        tabs.append(jnp.stack(rows))
    return jnp.stack(tabs)


def kernel(x_prompt, x_sample, cache_attn_k, cache_attn_v, cache_mla_ckv, cache_mla_kpe, state_ssm_re,
           state_ssm_im, c, c_ctx, norm_mix, norm_ffn, w_ada, b_ada, w_in, a_q_norm, a_k_norm, a_sink,
           ssm_a_re, ssm_a_im, ssm_log_dt, ssm_b_re, ssm_b_im, ssm_c_re, ssm_c_im, ssm_d, w_glu,
           mla_q_norm, mla_kv_norm, w_q_b, w_kv_b, mla_qh_norm, mla_kh_norm, w_out, w_router,
           router_bias, w_gate, w_up, w_down):
    x = jnp.concatenate([x_sample.reshape(T_LAT, D), x_prompt.reshape(T_CTX, D)], axis=0)

    cond = jnp.concatenate([c_ctx[None, :], c, jnp.zeros((16 - 1 - N_LAT_B, D), F32)], axis=0)
    mod = _adaln(cond, w_ada, b_ada)

    pw_re, pw_im, bbt_re, bbt_im = _discretize(ssm_a_re, ssm_a_im, ssm_log_dt, ssm_b_re, ssm_b_im)
    pw_re = pw_re.reshape(DEPTH, 2, SSM_G, 8, SSM_P)
    pw_im = pw_im.reshape(DEPTH, 2, SSM_G, 8, SSM_P)
    bbt_re = bbt_re.reshape(DEPTH, 2, SSM_G, SSM_H, SSM_P)
    bbt_im = bbt_im.reshape(DEPTH, 2, SSM_G, SSM_H, SSM_P)
    h0_ctx = jnp.zeros((2, N_CTX_B, 2 * SSM_N), F32)

    tab_a, tab_c = _rope_tables()

    wr_t = jnp.pad(w_router.T, ((0, LANES - N_EXP), (0, 0)))
    wr_hi = wr_t.astype(BF16)
    wr_lo = (wr_t - wr_hi.astype(F32)).astype(BF16)
    rb = jnp.broadcast_to(router_bias[:, None], (N_EXP, LANES))

    w_kv4 = w_kv_b.reshape(DEPTH, C_KVLORA, C_HEADS, C_NOPE + C_V)
    wkn_all = jnp.pad(w_kv4[..., :C_NOPE], ((0, 0), (0, 0), (0, 0), (0, LANES - C_NOPE))
                      ).reshape(DEPTH, C_KVLORA, C_HEADS * LANES).astype(BF16)
    wv_all = w_kv4[..., C_NOPE:].reshape(DEPTH, C_KVLORA, C_HEADS * C_V).astype(BF16)
    gkh_all = jnp.pad(mla_kh_norm, ((0, 0), (0, LANES - C_QK))).reshape(DEPTH, 1, LANES)

    kpe_slab = jnp.pad(cache_mla_kpe, ((0, 0), (0, 0), (0, 0), (C_NOPE, LANES - C_QK)))
    kx_all, vx_all = _cache_kv(cache_mla_ckv, kpe_slab, wkn_all, wv_all, gkh_all)

    cache_k = cache_attn_k.reshape(N_LAT_B, DEPTH, CTX_L, A_KV * A_DH)
    cache_v = cache_attn_v.reshape(N_LAT_B, DEPTH, CTX_L, A_KV * A_DH)

    wg_bf, wu_bf, wd_bf = w_gate.astype(BF16), w_up.astype(BF16), w_down.astype(BF16)

    ks, vs, ckvs, kpes, fins = [], [], [], [], []
    for l in range(DEPTH):
        wl = w_in[l]
        w = {
            "gmix": norm_mix[l][None, :],
            "wq": wl[:, 0:512].astype(BF16),
            "wkv": wl[:, 512:768].astype(BF16),
            "wu": wl[:, 768:1024].astype(BF16),
            "wcq": wl[:, 1024:1280].astype(BF16),
            "wt": jnp.pad(wl[:, 1280:1440], ((0, 0), (0, 256 - 160))).astype(BF16),
            "gq": jnp.tile(a_q_norm[l], 2)[None, :],
            "gk": jnp.tile(a_k_norm[l], 2)[None, :],
            "gcq": mla_q_norm[l][None, :],
            "gckv": mla_kv_norm[l][None, :],
            "wqb": _pad_heads(w_q_b[l], C_HEADS, C_QK).astype(BF16),
            "wkn": wkn_all[l],
            "wv": wv_all[l],
            "gqh": jnp.pad(mla_qh_norm[l], (0, LANES - C_QK))[None, :],
            "gkh": gkh_all[l],
            "ssm_d": ssm_d[l].reshape(1, SSM_W),
            "wglu": w_glu[l].astype(BF16),
            "woa": w_out[l, 0:512].astype(BF16),
            "wob": w_out[l, 512:768].astype(BF16),
            "woc": w_out[l, 768:1024].astype(BF16),
            "gffn": norm_ffn[l][None, :],
        }
        mod_l = mod[l]
        qa, kab, vaug, u, qc, kc, vc, ka_ctx, va_ctx, ckv_ctx, kpe_ctx = _proj(x, mod_l, w, tab_a, tab_c)

        sink = jnp.broadcast_to(a_sink[l][:, None], (A_HEADS, LANES))
        oa_ctx = _attn_a_ctx(qa, kab, vaug, sink)
        oa_lat = _attn_a_lat(qa, kab, vaug, cache_k[:, l], cache_v[:, l], sink)
        oc_ctx = _mla_ctx(qc, kc, vc)
        oc_lat = _mla_lat(qc, kc, vc, kx_all[l], vx_all[l])

        bmat = jnp.stack([jnp.concatenate([_block_diag_b(bbt_re[l, d]), _block_diag_b(bbt_im[l, d])], axis=1)
                          for d in range(2)]).astype(BF16)
        cmat = jnp.stack([jnp.concatenate([_block_diag_c(ssm_c_re[l, d]), -_block_diag_c(ssm_c_im[l, d])], axis=0)
                          for d in range(2)]).astype(BF16)
        tab = _scan_tables(pw_re[l], pw_im[l])
        yf_ctx, yb_ctx, fin = _ssm_ctx(u, bmat, cmat, tab, h0_ctx)
        h0 = jnp.swapaxes(jnp.concatenate([state_ssm_re[:, l].reshape(N_LAT_B, 2, SSM_N),
                                            state_ssm_im[:, l].reshape(N_LAT_B, 2, SSM_N)], axis=-1), 0, 1)
        yf_lat, yb_lat, _ = _ssm_lat(u, bmat, cmat, tab, h0)

        x1, h2, comb = _outproj(x, u, oa_lat, oa_ctx, oc_lat, oc_ctx,
                                yf_lat.reshape(T_LAT, SSM_W), yf_ctx.reshape(T_CTX, SSM_W),
                                yb_lat.reshape(T_LAT, SSM_W), yb_ctx.reshape(T_CTX, SSM_W),
                                mod_l, w, wr_hi, wr_lo, rb)
        x = _moe(h2, comb, x1, mod_l, wg_bf, wu_bf, wd_bf, l, split=(l == DEPTH - 1))

        ks.append(ka_ctx.reshape(N_CTX_B, CTX_L, A_KV, A_DH))
        vs.append(va_ctx.reshape(N_CTX_B, CTX_L, A_KV, A_DH))
        ckvs.append(ckv_ctx.reshape(N_CTX_B, CTX_L, C_KVLORA))
        kpes.append(kpe_ctx.reshape(N_CTX_B, CTX_L, C_ROPE))
        fins.append(fin)

    y_sample = x[0].reshape(N_LAT_B, LAT_L, D)
    y_prompt = x[1].reshape(N_CTX_B, CTX_L, D)
    fin = jnp.transpose(jnp.stack(fins, axis=0), (2, 0, 1, 3))
    new_re = fin[..., 0:SSM_N].reshape(N_CTX_B, DEPTH, 2, SSM_G, SSM_P)
    new_im = fin[..., SSM_N:].reshape(N_CTX_B, DEPTH, 2, SSM_G, SSM_P)
    return (y_prompt, y_sample, jnp.stack(ks, axis=1), jnp.stack(vs, axis=1), jnp.stack(ckvs, axis=1),
            jnp.stack(kpes, axis=1), new_re, new_im)
```
